```python
import math
import jax, jax.numpy as jnp
from jax import lax
import numpy as np

D_MODEL = 4096
BATCH = 4
SEQ = 2048
DEPTH = 4
DEC_BATCH = 4
DEC_SEQ = 4096
PAST_LEN = 128

N_MOD = 9
D_FF = 256 * ((8 * D_MODEL // 3 + 255) // 256)
NORM_EPS = 1e-6
DA_WIDTH = D_MODEL // 4
DA_HEAD_DIM = 128
DA_HEADS = DA_WIDTH // DA_HEAD_DIM
DA_SUB_DIM = DA_HEAD_DIM // 2
ROT_DIM = DA_SUB_DIM // 4
ROPE_THETA = 500000.0
Q_BLOCK = 128
HG_WIDTH = D_MODEL // 4
HG_KDIM = 128
HG_HEADS = HG_WIDTH // HG_KDIM
HG_VDIM = HG_WIDTH // HG_HEADS
HG_FDIM = HG_HEADS * HG_KDIM
HG_CHUNK = 64
LB_FLOOR = 1e-6
S5_WIDTH = D_MODEL - DA_WIDTH - HG_WIDTH
S5_GROUP = 16
S5_GROUPS = S5_WIDTH // S5_GROUP
S5_STATE = 64
IN_SIZES = (DA_WIDTH, DA_WIDTH, DA_WIDTH, HG_FDIM, HG_FDIM, HG_FDIM, HG_WIDTH, HG_WIDTH, S5_WIDTH)
IN_WIDTH = sum(IN_SIZES)

kernel_name = 'hymba_style_diffattn_hgrn2_s5_macaron_encoder'

F32 = jnp.float32


def _rmsnorm(x, g):
    x32 = x.astype(F32)
    y = x32 * lax.rsqrt(jnp.mean(x32 * x32, axis=-1, keepdims=True) + NORM_EPS)
    return (y * g.astype(F32)).astype(x.dtype)


def _modulate(h, shift, scale):
    return h * (1.0 + scale) + shift


def _swiglu(h, wg, wu, wd):
    return jnp.dot(jax.nn.silu(jnp.dot(h, wg)) * jnp.dot(h, wu), wd)


def _rope_tables(L):
    inv = ROPE_THETA ** (-jnp.arange(0, ROT_DIM, 2, dtype=F32) / ROT_DIM)
    ang = jnp.arange(L, dtype=F32)[:, None] * inv[None, :]
    return jnp.cos(ang), jnp.sin(ang)


def _partial_rope(x, cos, sin):
    xf = x.astype(F32)
    half = ROT_DIM // 2
    x1, x2 = xf[..., :half], xf[..., half:ROT_DIM]
    c, s = cos[None, :, None, :], sin[None, :, None, :]
    return jnp.concatenate([x1 * c - x2 * s, x2 * c + x1 * s, xf[..., ROT_DIM:]], axis=-1).astype(x.dtype)


def _diff_attention(q, k, v, lam):
    b, L = q.shape[0], q.shape[1]
    nblk = L // Q_BLOCK
    q_blocks = jnp.moveaxis(q.reshape(b, nblk, Q_BLOCK, 2 * DA_HEADS, DA_SUB_DIM), 1, 0)
    scale = DA_SUB_DIM ** -0.5

    def one_block(qb):
        s = jnp.einsum('bqhd,bkhd->bhqk', qb, k).astype(F32) * scale
        p = jax.nn.softmax(s, axis=-1).reshape(b, DA_HEADS, 2, Q_BLOCK, L)
        w = (p[:, :, 0] - lam * p[:, :, 1]).astype(v.dtype)
        return jnp.einsum('bhqk,bkhv->bqhv', w, v)

    o = lax.map(one_block, q_blocks)
    return jnp.moveaxis(o, 0, 1).reshape(b, L, DA_HEADS, DA_HEAD_DIM)


def _gla_chunk_scan(q, k, v, logf):
    b, h, L, kd = q.shape
    vd = v.shape[-1]
    n = L // HG_CHUNK

    def to_chunks(a):
        return jnp.moveaxis(a.reshape(b, h, n, HG_CHUNK, a.shape[-1]), 2, 0)

    qc, kc, vc, gc = to_chunks(q), to_chunks(k), to_chunks(v), to_chunks(logf)
    tri = jnp.tril(jnp.ones((HG_CHUNK, HG_CHUNK), dtype=bool))[:, :, None]

    def step(S, inp):
        qb, kb, vb, gb = inp
        cum = jnp.cumsum(gb, axis=2)
        diff = cum[:, :, :, None, :] - cum[:, :, None, :, :]
        decay = jnp.where(tri, jnp.exp(jnp.where(tri, diff, 0.0)), 0.0)
        att = jnp.einsum('bhtk,bhsk,bhtsk->bhts', qb, kb, decay)
        o = jnp.einsum('bhts,bhsv->bhtv', att, vb) + jnp.einsum('bhtk,bhkv->bhtv', qb * jnp.exp(cum), S)
        last = cum[:, :, -1, :]
        rel = jnp.minimum(last[:, :, None, :] - cum, 0.0)
        S = jnp.exp(last)[..., None] * S + jnp.einsum('bhsk,bhsv->bhkv', kb * jnp.exp(rel), vb)
        return S, o

    S0 = jnp.zeros((b, h, kd, vd), F32)
    _, o = lax.scan(step, S0, (qc, kc, vc, gc))
    return jnp.moveaxis(o, 0, 2).reshape(b, h, L, vd)


def _hgrn2(qh, zf_f, zf_b, ih, gh, lb_f, lb_b, gn_g):
    b, L = qh.shape[0], qh.shape[1]

    def heads(a, d):
        return jnp.transpose(a.reshape(b, L, HG_HEADS, d), (0, 2, 1, 3))

    q = heads(jax.nn.silu(qh.astype(F32)), HG_KDIM) * HG_KDIM ** -0.5
    v = heads(ih.astype(F32), HG_VDIM)

    def gates(zf, lb):
        zf = zf.astype(F32)
        lb = jnp.clip(lb.astype(F32), 0.0, 1.0 - LB_FLOOR)
        logf = jnp.logaddexp(jnp.log(jnp.maximum(lb, LB_FLOOR)), jnp.log1p(-lb) + jax.nn.log_sigmoid(zf))
        logf = jnp.minimum(logf, 0.0)
        kin = (1.0 - lb) * jax.nn.sigmoid(-zf)
        return heads(logf, HG_KDIM), heads(kin, HG_KDIM)

    logf_f, k_f = gates(zf_f, lb_f)
    logf_b, k_b = gates(zf_b, lb_b)
    o_fwd = _gla_chunk_scan(q, k_f, v, logf_f)
    fl = lambda a: jnp.flip(a, axis=2)
    o_bwd = fl(_gla_chunk_scan(fl(q), fl(k_b), fl(v), fl(logf_b)))
    o = jnp.transpose(o_fwd + o_bwd, (0, 2, 1, 3))
    o = _rmsnorm(o, gn_g) * jax.nn.silu(gh.astype(F32).reshape(b, L, HG_HEADS, HG_VDIM))
    return o.reshape(b, L, HG_WIDTH)


def _complex_affine_combine(e1, e2):
    a1r, a1i, b1r, b1i = e1
    a2r, a2i, b2r, b2i = e2
    return (a1r * a2r - a1i * a2i,
            a1r * a2i + a1i * a2r,
            a2r * b1r - a2i * b1i + b2r,
            a2r * b1i + a2i * b1r + b2i)


def _s5_scan(u, lam_re, lam_im, log_dt, b_re, b_im, c_re, c_im):
    L, bsz = u.shape[1], u.shape[0]
    lr = jnp.minimum(lam_re.astype(F32), -1e-4)
    li = lam_im.astype(F32)
    dt = jnp.exp(log_dt.astype(F32))[:, None]
    mag = jnp.exp(lr * dt)
    ar, ai = mag * jnp.cos(li * dt), mag * jnp.sin(li * dt)
    den = lr * lr + li * li
    er, ei = ar - 1.0, ai
    zr, zi = (er * lr + ei * li) / den, (ei * lr - er * li) / den
    br, bi = b_re.astype(F32), b_im.astype(F32)
    bbr = zr[..., None] * br - zi[..., None] * bi
    bbi = zr[..., None] * bi + zi[..., None] * br
    ut = jnp.moveaxis(u, 1, 0)
    wr = jnp.einsum('lbgn,gpn->lbgp', ut, bbr)
    wi = jnp.einsum('lbgn,gpn->lbgp', ut, bbi)
    a_r = jnp.broadcast_to(ar[None, None], (L, bsz) + ar.shape)
    a_i = jnp.broadcast_to(ai[None, None], (L, bsz) + ai.shape)
    _, _, xr, xi = lax.associative_scan(_complex_affine_combine, (a_r, a_i, wr, wi), axis=0)
    return (jnp.einsum('lbgp,gnp->blgn', xr, c_re.astype(F32))
            - jnp.einsum('lbgp,gnp->blgn', xi, c_im.astype(F32)))


def _mixer(h, l, p):
    b, L, _ = h.shape
    z = jnp.dot(h, p['w_in'][l])
    offs = np.cumsum(IN_SIZES)[:-1].tolist()
    qa, ka, va, qh, zf_f, zf_b, ih, gh, us = jnp.split(z, offs, axis=-1)

    q = _rmsnorm(qa.reshape(b, L, 2 * DA_HEADS, DA_SUB_DIM), p['qk_norm_g'][l, 0])
    k = _rmsnorm(ka.reshape(b, L, 2 * DA_HEADS, DA_SUB_DIM), p['qk_norm_g'][l, 1])
    cos, sin = _rope_tables(L)
    q, k = _partial_rope(q, cos, sin), _partial_rope(k, cos, sin)
    v = va.reshape(b, L, DA_HEADS, DA_HEAD_DIM)
    lam_init = 0.8 - 0.6 * math.exp(-0.3 * l)
    lv = p['diff_lambda'][l].astype(F32)
    lam = jnp.exp(jnp.sum(lv[0] * lv[1])) - jnp.exp(jnp.sum(lv[2] * lv[3])) + lam_init
    o_da = _diff_attention(q, k, v, lam)
    o_da = (_rmsnorm(o_da, p['diff_subln_g'][l]) * (1.0 - lam_init)).reshape(b, L, DA_WIDTH)

    lbs = jax.nn.softmax(p['hg_lb_logits'].astype(F32), axis=1)
    lbs = jnp.cumsum(lbs, axis=1) - lbs[:, :1]
    o_hg = _hgrn2(qh, zf_f, zf_b, ih, gh, lbs[0, l], lbs[1, l], p['hg_gnorm_g'][l])

    u = us.reshape(b, L, S5_GROUPS, S5_GROUP).astype(F32)
    names = ('s5_lam_re', 's5_lam_im', 's5_log_dt', 's5_b_re', 's5_b_im', 's5_c_re', 's5_c_im')
    fwd = [p[n][l, 0] for n in names]
    bwd = [p[n][l, 1] for n in names]
    y = (u * p['s5_d'][l].astype(F32).reshape(S5_GROUPS, S5_GROUP)
         + _s5_scan(u, *fwd)
         + jnp.flip(_s5_scan(jnp.flip(u, axis=1), *bwd), axis=1))
    y = jax.nn.gelu(y.reshape(b, L, S5_WIDTH)).astype(h.dtype)
    o_s5 = y * jax.nn.sigmoid(jnp.dot(y, p['w_glu'][l]) + p['b_glu'][l])

    mix = jnp.concatenate([o_da.astype(h.dtype), o_hg.astype(h.dtype), o_s5.astype(h.dtype)], axis=-1)
    return jnp.dot(mix, p['w_out'][l])


def _layer(x, c, l, p):
    mod = jnp.dot(jax.nn.silu(c), p['w_ada'][l]) + p['b_ada'][l]
    mod = mod.reshape(c.shape[0], 1, N_MOD, D_MODEL)
    g = p['norm_g'][l]
    h = _modulate(_rmsnorm(x, g[0]), mod[:, :, 0], mod[:, :, 1])
    x = x + 0.5 * mod[:, :, 2] * _swiglu(h, p['w_ff_gate'][l, 0], p['w_ff_up'][l, 0], p['w_ff_down'][l, 0])
    h = _modulate(_rmsnorm(x, g[1]), mod[:, :, 3], mod[:, :, 4])
    x = x + mod[:, :, 5] * _mixer(h, l, p)
    h = _modulate(_rmsnorm(x, g[2]), mod[:, :, 6], mod[:, :, 7])
    x = x + 0.5 * mod[:, :, 8] * _swiglu(h, p['w_ff_gate'][l, 1], p['w_ff_up'][l, 1], p['w_ff_down'][l, 1])
    return _rmsnorm(x, g[3])


def _trunk(x, c, p):
    for l in range(DEPTH):
        x = _layer(x, c, l, p)
    return x


def setup_inputs(seed: int = 0) -> dict:
    key = jax.random.key(seed)
    ks = jax.random.split(key, 32)
    d = D_MODEL
    G, P, N = S5_GROUPS, S5_STATE, S5_GROUP

    def nrm(k, shape, s):
        return jax.random.normal(k, shape, F32) * s

    lam_im0 = math.pi * jnp.arange(P, dtype=F32)
    return {
        'x_prompt': nrm(ks[0], (BATCH, SEQ, d), 1.0),
        'x_sample': nrm(ks[1], (DEC_BATCH, DEC_SEQ, d), 1.0),
        'c_prompt': nrm(ks[2], (BATCH, d), 1.0),
        'c_sample': nrm(ks[3], (DEC_BATCH, d), 1.0),
        'w_ada': nrm(ks[4], (DEPTH, d, N_MOD * d), 0.5 * d ** -0.5),
        'b_ada': nrm(ks[5], (DEPTH, N_MOD * d), 0.02),
        'norm_g': 1.0 + nrm(ks[6], (DEPTH, 4, d), 0.02),
        'w_ff_gate': nrm(ks[7], (DEPTH, 2, d, D_FF), d ** -0.5),
        'w_ff_up': nrm(ks[8], (DEPTH, 2, d, D_FF), d ** -0.5),
        'w_ff_down': nrm(ks[9], (DEPTH, 2, D_FF, d), D_FF ** -0.5),
        'w_in': nrm(ks[10], (DEPTH, d, IN_WIDTH), d ** -0.5),
        'w_out': nrm(ks[11], (DEPTH, d, d), d ** -0.5),
        'qk_norm_g': 1.0 + nrm(ks[12], (DEPTH, 2, DA_SUB_DIM), 0.02),
        'diff_lambda': nrm(ks[13], (DEPTH, 4, DA_SUB_DIM), 0.1),
        'diff_subln_g': 1.0 + nrm(ks[14], (DEPTH, DA_HEAD_DIM), 0.02),
        'hg_lb_logits': nrm(ks[15], (2, DEPTH, HG_FDIM), 0.1),
        'hg_gnorm_g': 1.0 + nrm(ks[16], (DEPTH, HG_VDIM), 0.02),
        's5_lam_re': -0.5 + nrm(ks[17], (DEPTH, 2, G, P), 0.01),
        's5_lam_im': lam_im0 + nrm(ks[18], (DEPTH, 2, G, P), 0.01),
        's5_log_dt': jax.random.uniform(ks[19], (DEPTH, 2, G), F32, math.log(1e-3), math.log(1e-1)),
        's5_b_re': nrm(ks[20], (DEPTH, 2, G, P, N), (2 * N) ** -0.5),
        's5_b_im': nrm(ks[21], (DEPTH, 2, G, P, N), (2 * N) ** -0.5),
        's5_c_re': nrm(ks[22], (DEPTH, 2, G, N, P), 0.5),
        's5_c_im': nrm(ks[23], (DEPTH, 2, G, N, P), 0.5),
        's5_d': nrm(ks[24], (DEPTH, S5_WIDTH), 1.0),
        'w_glu': nrm(ks[25], (DEPTH, S5_WIDTH, S5_WIDTH), S5_WIDTH ** -0.5),
        'b_glu': nrm(ks[26], (DEPTH, S5_WIDTH), 0.02),
    }


def reference(x_prompt, x_sample, c_prompt, c_sample, w_ada, b_ada, norm_g, w_ff_gate, w_ff_up,
              w_ff_down, w_in, w_out, qk_norm_g, diff_lambda, diff_subln_g, hg_lb_logits, hg_gnorm_g,
              s5_lam_re, s5_lam_im, s5_log_dt, s5_b_re, s5_b_im, s5_c_re, s5_c_im, s5_d, w_glu, b_glu):
    p = dict(w_ada=w_ada, b_ada=b_ada, norm_g=norm_g, w_ff_gate=w_ff_gate, w_ff_up=w_ff_up,
             w_ff_down=w_ff_down, w_in=w_in, w_out=w_out, qk_norm_g=qk_norm_g,
             diff_lambda=diff_lambda, diff_subln_g=diff_subln_g, hg_lb_logits=hg_lb_logits,
             hg_gnorm_g=hg_gnorm_g, s5_lam_re=s5_lam_re, s5_lam_im=s5_lam_im, s5_log_dt=s5_log_dt,
             s5_b_re=s5_b_re, s5_b_im=s5_b_im, s5_c_re=s5_c_re, s5_c_im=s5_c_im, s5_d=s5_d,
             w_glu=w_glu, b_glu=b_glu)
    y_prompt = _trunk(x_prompt, c_prompt, p)
    y_sample = _trunk(x_sample, c_sample, p)
    return (y_prompt, y_sample)
```

```python
import functools
import math

import jax
import jax.numpy as jnp
import numpy as np
from jax import lax
from jax.experimental import pallas as pl
from jax.experimental.pallas import tpu as pltpu

F32 = jnp.float32
BF16 = jnp.bfloat16
HIGHEST = lax.Precision.HIGHEST

V7X_LANES = 128
V7X_SUBLANES = 8
V7X_VMEM_LIMIT = 56 * 1024 * 1024

NORM_EPS = 1e-6
N_MOD = 9
DA_HEADS = 8
DA_HEAD_DIM = 128
DA_SUB_DIM = 64
ROT_DIM = 16
ROPE_THETA = 500000.0
HG_HEADS = 8
HG_DIM = 128
LB_FLOOR = 1e-6
GLA_CHUNK = 64
GLA_DIAG = 8
S5_GROUP = 16
S5_GROUPS = 128
S5_STATE = 64
S5_CHUNK = 16


def _params(sem, vmem=V7X_VMEM_LIMIT):
    return pltpu.CompilerParams(dimension_semantics=sem, vmem_limit_bytes=vmem)


def _dot(a, b):
    return jnp.dot(a, b, preferred_element_type=F32)


def _dot_nt(a, b):
    return lax.dot_general(a, b, (((1,), (1,)), ((), ())), preferred_element_type=F32)


def _dot_tn(a, b):
    return lax.dot_general(a, b, (((0,), (0,)), ((), ())), preferred_element_type=F32)


def _sigmoid(x):
    return 1.0 / (1.0 + jnp.exp(-x))


def _ada_kernel(c_ref, w_ref, b_ref, o_ref):
    c = c_ref[...]
    a = c * _sigmoid(c)
    o_ref[...] = jnp.dot(a, w_ref[...], preferred_element_type=F32, precision=HIGHEST) + b_ref[...]


def _ada(c, w_ada, b_ada3, layer, tn=512):
    s, d = c.shape
    n = w_ada.shape[-1]
    return pl.pallas_call(
        _ada_kernel,
        grid=(n // tn,),
        in_specs=[
            pl.BlockSpec((s, d), lambda j: (0, 0)),
            pl.BlockSpec((None, d, tn), lambda j: (layer, 0, j)),
            pl.BlockSpec((None, 1, tn), lambda j: (layer, 0, j)),
        ],
        out_specs=pl.BlockSpec((s, tn), lambda j: (0, j)),
        out_shape=jax.ShapeDtypeStruct((s, n), F32),
        compiler_params=_params(("arbitrary",)),
        name="ada",
    )(c, w_ada, b_ada3)


def _rms(x):
    return x * lax.rsqrt(jnp.mean(x * x, axis=-1, keepdims=True) + NORM_EPS)


def _normmod_kernel(x_ref, g_ref, sh_ref, sc_ref, o_ref):
    y = _rms(x_ref[...]) * g_ref[...]
    o_ref[...] = (y * (1.0 + sc_ref[...]) + sh_ref[...]).astype(o_ref.dtype)


def _normmod(x, norm_g3, gi, mod3, seq0, rows_per_seq, j_shift, j_scale, tm=256):
    t, d = x.shape
    tm = min(tm, rows_per_seq)
    bps = rows_per_seq // tm
    return pl.pallas_call(
        _normmod_kernel,
        grid=(t // tm,),
        in_specs=[
            pl.BlockSpec((tm, d), lambda i: (i, 0)),
            pl.BlockSpec((None, 1, d), lambda i: (gi, 0, 0)),
            pl.BlockSpec((None, 1, d), lambda i: ((seq0 + i // bps) * N_MOD + j_shift, 0, 0)),
            pl.BlockSpec((None, 1, d), lambda i: ((seq0 + i // bps) * N_MOD + j_scale, 0, 0)),
        ],
        out_specs=pl.BlockSpec((tm, d), lambda i: (i, 0)),
        out_shape=jax.ShapeDtypeStruct((t, d), BF16),
        compiler_params=_params(("parallel",)),
        name="normmod",
    )(x, norm_g3, mod3, mod3)


def _norm_kernel(x_ref, g_ref, o_ref):
    o_ref[...] = _rms(x_ref[...]) * g_ref[...]


def _norm(x, norm_g3, gi, tm=256):
    t, d = x.shape
    tm = min(tm, t)
    return pl.pallas_call(
        _norm_kernel,
        grid=(t // tm,),
        in_specs=[
            pl.BlockSpec((tm, d), lambda i: (i, 0)),
            pl.BlockSpec((None, 1, d), lambda i: (gi, 0, 0)),
        ],
        out_specs=pl.BlockSpec((tm, d), lambda i: (i, 0)),
        out_shape=jax.ShapeDtypeStruct((t, d), F32),
        compiler_params=_params(("parallel",)),
        name="norm",
    )(x, norm_g3)


def _ffup_kernel(h_ref, wg_ref, wu_ref, o_ref):
    h = h_ref[...]
    g = _dot(h, wg_ref[...])
    u = _dot(h, wu_ref[...])
    o_ref[...] = (g * _sigmoid(g) * u).astype(o_ref.dtype)


def _ffup(h, wg, wu, li, tm=1024, tf=256):
    t, d = h.shape
    f = wg.shape[-1]
    tm, tf = min(tm, t), min(tf, f)
    return pl.pallas_call(
        _ffup_kernel,
        grid=(t // tm, f // tf),
        in_specs=[
            pl.BlockSpec((tm, d), lambda i, j: (i, 0)),
            pl.BlockSpec((None, d, tf), lambda i, j: (li, 0, j)),
            pl.BlockSpec((None, d, tf), lambda i, j: (li, 0, j)),
        ],
        out_specs=pl.BlockSpec((tm, tf), lambda i, j: (i, j)),
        out_shape=jax.ShapeDtypeStruct((t, f), BF16),
        compiler_params=_params(("parallel", "arbitrary")),
        name="ffup",
    )(h, wg, wu)


def _ffdown_kernel(a_ref, w_ref, x_ref, gate_ref, o_ref, *, scale):
    acc = _dot(a_ref[...], w_ref[...])
    o_ref[...] = x_ref[...] + (scale * gate_ref[...]) * acc


def _ffdown(act, wd, li, x, mod3, seq0, rows_per_seq, j_gate, scale, tm=512, tn=256):
    t, f = act.shape
    d = wd.shape[-1]
    tm, tn = min(tm, rows_per_seq), min(tn, d)
    bps = rows_per_seq // tm
    return pl.pallas_call(
        functools.partial(_ffdown_kernel, scale=scale),
        grid=(t // tm, d // tn),
        in_specs=[
            pl.BlockSpec((tm, f), lambda i, j: (i, 0)),
            pl.BlockSpec((None, f, tn), lambda i, j: (li, 0, j)),
            pl.BlockSpec((tm, tn), lambda i, j: (i, j)),
            pl.BlockSpec((None, 1, tn), lambda i, j: ((seq0 + i // bps) * N_MOD + j_gate, 0, j)),
        ],
        out_specs=pl.BlockSpec((tm, tn), lambda i, j: (i, j)),
        out_shape=jax.ShapeDtypeStruct((t, d), F32),
        compiler_params=_params(("parallel", "arbitrary")),
        name="ffdown",
    )(act, wd, x, mod3)


def _mm_kernel(a_ref, w_ref, o_ref):
    o_ref[...] = _dot(a_ref[...], w_ref[...]).astype(o_ref.dtype)


def _mm_in(h, w_in, layer, tm=1024, tn=512):
    t, d = h.shape
    n = w_in.shape[-1]
    tm, tn = min(tm, t), min(tn, n)
    return pl.pallas_call(
        _mm_kernel,
        grid=(t // tm, n // tn),
        in_specs=[
            pl.BlockSpec((tm, d), lambda i, j: (i, 0)),
            pl.BlockSpec((None, d, tn), lambda i, j: (layer, 0, j)),
        ],
        out_specs=pl.BlockSpec((tm, tn), lambda i, j: (i, j)),
        out_shape=jax.ShapeDtypeStruct((t, n), F32),
        compiler_params=_params(("parallel", "arbitrary")),
        name="mm_in",
    )(h, w_in)


def _mixout_kernel(da_ref, hg_ref, s5_ref, w_ref, x_ref, gate_ref, o_ref, *, w_da, w_hg):
    acc = _dot(da_ref[...], w_ref[0:w_da, :])
    acc = acc + _dot(hg_ref[...], w_ref[w_da:w_da + w_hg, :])
    acc = acc + _dot(s5_ref[...], w_ref[w_da + w_hg:, :])
    o_ref[...] = x_ref[...] + gate_ref[...] * acc


def _mixout(o_da, o_hg, o_s5, w_out, layer, x, mod3, seq0, rows_per_seq, j_gate, tm=1024, tn=512):
    t, w_da = o_da.shape
    w_hg, w_s5 = o_hg.shape[1], o_s5.shape[1]
    d_in, d = w_out.shape[1], w_out.shape[2]
    tm, tn = min(tm, rows_per_seq), min(tn, d)
    bps = rows_per_seq // tm
    return pl.pallas_call(
        functools.partial(_mixout_kernel, w_da=w_da, w_hg=w_hg),
        grid=(t // tm, d // tn),
        in_specs=[
            pl.BlockSpec((tm, w_da), lambda i, j: (i, 0)),
            pl.BlockSpec((tm, w_hg), lambda i, j: (i, 0)),
            pl.BlockSpec((tm, w_s5), lambda i, j: (i, 0)),
            pl.BlockSpec((None, d_in, tn), lambda i, j: (layer, 0, j)),
            pl.BlockSpec((tm, tn), lambda i, j: (i, j)),
            pl.BlockSpec((None, 1, tn), lambda i, j: ((seq0 + i // bps) * N_MOD + j_gate, 0, j)),
        ],
        out_specs=pl.BlockSpec((tm, tn), lambda i, j: (i, j)),
        out_shape=jax.ShapeDtypeStruct((t, d), F32),
        compiler_params=_params(("parallel", "arbitrary")),
        name="mixout",
    )(o_da, o_hg, o_s5, w_out, x, mod3)


def _glu_kernel(y_ref, w_ref, b_ref, yt_ref, o_ref):
    acc = _dot(y_ref[...], w_ref[...]) + b_ref[...]
    o_ref[...] = (yt_ref[...].astype(F32) * _sigmoid(acc)).astype(o_ref.dtype)


def _glu(y, w_glu, b_glu3, layer, tm=1024, tn=512):
    t, d = y.shape
    tm, tn = min(tm, t), min(tn, d)
    return pl.pallas_call(
        _glu_kernel,
        grid=(t // tm, d // tn),
        in_specs=[
            pl.BlockSpec((tm, d), lambda i, j: (i, 0)),
            pl.BlockSpec((None, d, tn), lambda i, j: (layer, 0, j)),
            pl.BlockSpec((None, 1, tn), lambda i, j: (layer, 0, j)),
            pl.BlockSpec((tm, tn), lambda i, j: (i, j)),
        ],
        out_specs=pl.BlockSpec((tm, tn), lambda i, j: (i, j)),
        out_shape=jax.ShapeDtypeStruct((t, d), BF16),
        compiler_params=_params(("parallel", "arbitrary")),
        name="glu",
    )(y, w_glu, b_glu3, y)


def _qkv_kernel(z_ref, g_ref, c_ref, s1_ref, s2_ref, o_ref):
    j = pl.program_id(1)

    @pl.when(j < 2 * DA_HEADS)
    def _():
        x = z_ref[...]
        lane = lax.broadcasted_iota(jnp.int32, x.shape, 1)
        lo = lane < DA_SUB_DIM
        x2 = x * x
        s_lo = jnp.sum(jnp.where(lo, x2, 0.0), axis=-1, keepdims=True)
        s_hi = jnp.sum(jnp.where(lo, 0.0, x2), axis=-1, keepdims=True)
        ms = jnp.where(lo, s_lo, s_hi) * (1.0 / DA_SUB_DIM)
        y = x * lax.rsqrt(ms + NORM_EPS) * g_ref[...]
        half = ROT_DIM // 2
        y_next = pltpu.roll(y, V7X_LANES - half, axis=1)
        y_prev = pltpu.roll(y, half, axis=1)
        o_ref[...] = (y * c_ref[...] + y_next * s1_ref[...] + y_prev * s2_ref[...]).astype(o_ref.dtype)

    @pl.when(j >= 2 * DA_HEADS)
    def _():
        o_ref[...] = z_ref[...].astype(o_ref.dtype)


def _qkv_prep(z, gains, rope_c, rope_s1, rope_s2, seq_len, tm=512):
    t = z.shape[0]
    tm = min(tm, seq_len)
    bps = seq_len // tm
    nblk = 3 * DA_HEADS
    return pl.pallas_call(
        _qkv_kernel,
        grid=(t // tm, nblk),
        in_specs=[
            pl.BlockSpec((tm, V7X_LANES), lambda i, j: (i, j)),
            pl.BlockSpec((None, 1, V7X_LANES), lambda i, j: (j, 0, 0)),
            pl.BlockSpec((tm, V7X_LANES), lambda i, j: (i % bps, 0)),
            pl.BlockSpec((tm, V7X_LANES), lambda i, j: (i % bps, 0)),
            pl.BlockSpec((tm, V7X_LANES), lambda i, j: (i % bps, 0)),
        ],
        out_specs=pl.BlockSpec((tm, V7X_LANES), lambda i, j: (i, j)),
        out_shape=jax.ShapeDtypeStruct((t, nblk * V7X_LANES), BF16),
        compiler_params=_params(("parallel", "arbitrary")),
        name="qkv_prep",
    )(z, gains, rope_c, rope_s1, rope_s2)


def _attn_kernel(q_ref, k_ref, v_ref, lam_ref, g_ref, o_ref, m_ref, l_ref, acc_ref, *, tk, nk):
    q = q_ref[...]
    tq = q.shape[0]
    lane = lax.broadcasted_iota(jnp.int32, q.shape, 1)
    lo = lane < DA_SUB_DIM
    zero = jnp.zeros_like(q)
    q2 = jnp.concatenate([jnp.where(lo, q, zero), jnp.where(lo, zero, q)], axis=0)
    m_ref[...] = jnp.full(m_ref.shape, -jnp.inf, F32)
    l_ref[...] = jnp.zeros(l_ref.shape, F32)
    acc_ref[...] = jnp.zeros(acc_ref.shape, F32)

    def body(j, carry):
        rows = pl.ds(pl.multiple_of(j * tk, tk), tk)
        s = _dot_nt(q2, k_ref[rows, :])
        m_old = m_ref[...]
        m_new = jnp.maximum(m_old, jnp.max(s, axis=-1, keepdims=True))
        alpha = jnp.exp(m_old - m_new)
        p = jnp.exp(s - m_new)
        l_ref[...] = alpha * l_ref[...] + jnp.sum(p, axis=-1, keepdims=True)
        acc_ref[...] = alpha * acc_ref[...] + _dot(p.astype(BF16), v_ref[rows, :])
        m_ref[...] = m_new
        return carry

    lax.fori_loop(0, nk, body, 0)
    acc = acc_ref[...]
    l = l_ref[...]
    o = acc[:tq] / l[:tq] - lam_ref[...] * (acc[tq:] / l[tq:])
    o_ref[...] = (_rms(o) * g_ref[...]).astype(o_ref.dtype)


def _attention(qkv, lam_vec, subln_g, batch, seq_len, tq=256, tk=512):
    t = qkv.shape[0]
    tq, tk = min(tq, seq_len), min(tk, seq_len)
    nq = seq_len // tq
    return pl.pallas_call(
        functools.partial(_attn_kernel, tk=tk, nk=seq_len // tk),
        grid=(batch, DA_HEADS, nq),
        in_specs=[
            pl.BlockSpec((tq, V7X_LANES), lambda b, h, i: (b * nq + i, h)),
            pl.BlockSpec((seq_len, V7X_LANES), lambda b, h, i: (b, DA_HEADS + h)),
            pl.BlockSpec((seq_len, V7X_LANES), lambda b, h, i: (b, 2 * DA_HEADS + h)),
            pl.BlockSpec((1, V7X_LANES), lambda b, h, i: (0, 0)),
            pl.BlockSpec((1, V7X_LANES), lambda b, h, i: (0, 0)),
        ],
        out_specs=pl.BlockSpec((tq, V7X_LANES), lambda b, h, i: (b * nq + i, h)),
        out_shape=jax.ShapeDtypeStruct((t, DA_HEADS * DA_HEAD_DIM), BF16),
        scratch_shapes=[
            pltpu.VMEM((2 * tq, 1), F32),
            pltpu.VMEM((2 * tq, 1), F32),
            pltpu.VMEM((2 * tq, V7X_LANES), F32),
        ],
        compiler_params=_params(("parallel", "parallel", "arbitrary")),
        name="diff_attn",
    )(qkv, qkv, qkv, lam_vec, subln_g)


def _bcast_rows(cum, idxs, rows_each):
    parts = [jnp.broadcast_to(cum[i:i + 1, :], (rows_each, cum.shape[1])) for i in idxs]
    return parts[0] if len(parts) == 1 else jnp.concatenate(parts, axis=0)


def _split3(x):
    hi = x.astype(BF16)
    r1 = x - hi.astype(F32)
    mid = r1.astype(BF16)
    lo = (r1 - mid.astype(F32)).astype(BF16)
    return hi, mid, lo


def _gla_chunk(q, k, v, lf, st, rev):
    c = q.shape[0]
    row = lax.broadcasted_iota(jnp.int32, (c, c), 0)
    col = lax.broadcasted_iota(jnp.int32, (c, c), 1)
    rowv = lax.broadcasted_iota(jnp.int32, (c, 1), 0)
    tri = (col >= row) if rev else (col <= row)
    trib = jnp.where(tri, 1.0, 0.0).astype(BF16)
    hi, mid, lo = _split3(lf)
    cum = (_dot(trib, hi) + _dot(trib, mid)) + _dot(trib, lo)
    edge = cum[0:1, :] if rev else cum[c - 1:c, :]

    q_in = (q * jnp.exp(cum)).astype(BF16)
    k_st = (k * jnp.exp(edge - cum)).astype(BF16)
    vb = v.astype(BF16)
    o = _dot_nt(q_in, st.astype(BF16))
    st_new = st * jnp.exp(edge) + _dot_tn(vb, k_st)

    att = jnp.zeros((c, c), F32)
    m = GLA_DIAG
    while m < c:
        blk = 2 * m
        nb = c // blk
        idxs = [j * blk + (m if rev else m - 1) for j in range(nb)]
        bnd = _bcast_rows(cum, idxs, blk)
        late = (rowv & (blk - 1)) >= m
        q_side = jnp.logical_not(late) if rev else late
        e = jnp.exp(jnp.where(q_side, cum - bnd, bnd - cum))
        a = _dot_nt((q * e).astype(BF16), (k * e).astype(BF16))
        same = (row >> int(math.log2(blk))) == (col >> int(math.log2(blk)))
        r_late = (row & (blk - 1)) >= m
        c_late = (col & (blk - 1)) >= m
        if rev:
            msk = same & jnp.logical_not(r_late) & c_late
        else:
            msk = same & r_late & jnp.logical_not(c_late)
        att = jnp.where(msk, a, att)
        m = blk
    d = GLA_DIAG
    idxs = [j * d + (d // 2 if rev else d // 2 - 1) for j in range(c // d)]
    ref = _bcast_rows(cum, idxs, d)
    a = _dot_nt((q * jnp.exp(cum - ref)).astype(BF16), (k * jnp.exp(ref - cum)).astype(BF16))
    same = (row >> int(math.log2(d))) == (col >> int(math.log2(d)))
    att = jnp.where(same & tri, a, att)
    o = o + _dot(att.astype(BF16), vb)
    return o, st_new


def _hgrn_kernel(qh_ref, zf_ref, zb_ref, ih_ref, gh_ref, lbf_ref, lbb_ref, gn_ref, o_ref,
                 oacc_ref, st_ref, *, chunk, nc):
    def gates(z, lb):
        lbc = jnp.clip(lb, 0.0, 1.0 - LB_FLOOR)
        e = jnp.exp(-jnp.abs(z))
        r = 1.0 / (1.0 + e)
        pos = z >= 0.0
        sig = jnp.where(pos, r, e * r)
        nsig = jnp.where(pos, e * r, r)
        logf = jnp.minimum(jnp.log(jnp.maximum(lbc, LB_FLOOR) + (1.0 - lbc) * sig), 0.0)
        return logf, (1.0 - lbc) * nsig

    def load(i, z_ref, lb_ref):
        rows = pl.ds(pl.multiple_of(i * chunk, chunk), chunk)
        x = qh_ref[rows, :]
        q = x * _sigmoid(x) * (HG_DIM ** -0.5)
        lf, k = gates(z_ref[rows, :], lb_ref[...])
        return rows, q, k, ih_ref[rows, :], lf

    st_ref[...] = jnp.zeros(st_ref.shape, F32)

    def fwd(i, carry):
        rows, q, k, v, lf = load(i, zf_ref, lbf_ref)
        o, st = _gla_chunk(q, k, v, lf, st_ref[...], False)
        st_ref[...] = st
        oacc_ref[rows, :] = o
        return carry

    lax.fori_loop(0, nc, fwd, 0)
    st_ref[...] = jnp.zeros(st_ref.shape, F32)

    def bwd(i, carry):
        rows, q, k, v, lf = load(nc - 1 - i, zb_ref, lbb_ref)
        o, st = _gla_chunk(q, k, v, lf, st_ref[...], True)
        st_ref[...] = st
        o = o + oacc_ref[rows, :]
        g = gh_ref[rows, :]
        o_ref[rows, :] = (_rms(o) * gn_ref[...] * (g * _sigmoid(g))).astype(o_ref.dtype)
        return carry

    lax.fori_loop(0, nc, bwd, 0)


def _hgrn(z, lb_f, lb_b, gn_g, batch, seq_len, col0, chunk=GLA_CHUNK):
    t = z.shape[0]
    chunk = min(chunk, seq_len)
    h = HG_HEADS

    def zspec(k):
        return pl.BlockSpec((seq_len, V7X_LANES), lambda b, hh: (b, col0 + k * h + hh))

    lbspec = pl.BlockSpec((None, 1, V7X_LANES), lambda b, hh: (hh, 0, 0))
    return pl.pallas_call(
        functools.partial(_hgrn_kernel, chunk=chunk, nc=seq_len // chunk),
        grid=(batch, h),
        in_specs=[zspec(0), zspec(1), zspec(2), zspec(3), zspec(4), lbspec, lbspec,
                  pl.BlockSpec((1, V7X_LANES), lambda b, hh: (0, 0))],
        out_specs=pl.BlockSpec((seq_len, V7X_LANES), lambda b, hh: (b, hh)),
        out_shape=jax.ShapeDtypeStruct((t, h * HG_DIM), BF16),
        scratch_shapes=[pltpu.VMEM((seq_len, V7X_LANES), F32), pltpu.VMEM((HG_DIM, HG_DIM), F32)],
        compiler_params=_params(("parallel", "parallel")),
        name="hgrn2",
    )(z, z, z, z, z, lb_f, lb_b, gn_g)


def _gelu_tanh(x):
    return 0.5 * x * (1.0 + jnp.tanh(0.7978845608028654 * (x + 0.044715 * (x * x * x))))


def _s5_kernel(u_ref, t_ref, f_ref, e_ref, a_ref, y_ref, fs_ref, xs_ref, *, nc, nseq, tr):
    m = nc * nseq
    gw = u_ref.shape[2]

    def ucat(r0):
        return jnp.concatenate([u_ref[0, r0:r0 + tr, :], u_ref[1, r0:r0 + tr, :]], axis=-1)

    for r0 in range(0, m, tr):
        fs_ref[r0:r0 + tr, :] = _dot(ucat(r0), f_ref[...])

    w = V7X_LANES
    a = a_ref[...]
    afr, afi, abr, abi = (jnp.broadcast_to(a[i:i + 1, :], (nseq, w)) for i in range(4))

    def body(i, carry):
        xfr, xfi, xbr, xbi = carry
        rf = pl.ds(pl.multiple_of(i * nseq, nseq), nseq)
        rb = pl.ds(pl.multiple_of((nc - 1 - i) * nseq, nseq), nseq)
        xs_ref[rf, 0:w] = xfr
        xs_ref[rf, w:2 * w] = xfi
        xs_ref[rb, 2 * w:3 * w] = xbr
        xs_ref[rb, 3 * w:4 * w] = xbi
        nfr = afr * xfr - afi * xfi + fs_ref[rf, 0:w]
        nfi = afr * xfi + afi * xfr + fs_ref[rf, w:2 * w]
        nbr = abr * xbr - abi * xbi + fs_ref[rb, 2 * w:3 * w]
        nbi = abr * xbi + abi * xbr + fs_ref[rb, 3 * w:4 * w]
        return nfr, nfi, nbr, nbi

    z = jnp.zeros((nseq, w), F32)
    lax.fori_loop(0, nc, body, (z, z, z, z))

    for r0 in range(0, m, tr):
        y = _dot(ucat(r0), t_ref[...]) + _dot(xs_ref[r0:r0 + tr, :].astype(BF16), e_ref[...])
        y = _gelu_tanh(y).astype(y_ref.dtype)
        y_ref[0, r0:r0 + tr, :] = y[:, :gw]
        y_ref[1, r0:r0 + tr, :] = y[:, gw:]


def _s5(u, t_tab, f_tab, e_tab, a_tab, nc, nseq, tr=512):
    g, m, gw = u.shape
    tr = min(tr, m)
    pw = 2 * gw
    tab = pl.BlockSpec((None, pw, pw), lambda i: (i, 0, 0))
    return pl.pallas_call(
        functools.partial(_s5_kernel, nc=nc, nseq=nseq, tr=tr),
        grid=(g // 2,),
        in_specs=[
            pl.BlockSpec((2, m, gw), lambda i: (i, 0, 0)),
            tab, tab, tab,
            pl.BlockSpec((None, 4, V7X_LANES), lambda i: (i, 0, 0)),
        ],
        out_specs=pl.BlockSpec((2, m, gw), lambda i: (i, 0, 0)),
        out_shape=jax.ShapeDtypeStruct((g, m, gw), BF16),
        scratch_shapes=[pltpu.VMEM((m, pw), F32), pltpu.VMEM((m, pw), F32)],
        compiler_params=_params(("parallel",)),
        name="s5",
    )(u, t_tab, f_tab, e_tab, a_tab)


def _s5_tables(lam_re, lam_im, log_dt, b_re, b_im, c_re, c_im, d_skip):
    cs, n, p, g = S5_CHUNK, S5_GROUP, S5_STATE, S5_GROUPS
    lr = jnp.minimum(lam_re.astype(F32), -1e-4)
    li = lam_im.astype(F32)
    dt = jnp.exp(log_dt.astype(F32))[..., None]
    mag = jnp.exp(lr * dt)
    ar, ai = mag * jnp.cos(li * dt), mag * jnp.sin(li * dt)
    den = lr * lr + li * li
    er, ei = ar - 1.0, ai
    zr, zi = (er * lr + ei * li) / den, (ei * lr - er * li) / den
    br, bi = b_re.astype(F32), b_im.astype(F32)
    bbr = zr[..., None] * br - zi[..., None] * bi
    bbi = zr[..., None] * bi + zi[..., None] * br
    cr, ci = c_re.astype(F32), c_im.astype(F32)
    tau = jnp.arange(cs + 1, dtype=F32)[:, None, None, None]
    pmag = jnp.exp(lr[None] * dt[None] * tau)
    pr = pmag * jnp.cos(li[None] * dt[None] * tau)
    pi = pmag * jnp.sin(li[None] * dt[None] * tau)
    car = cr[None] * pr[:, :, :, None, :] - ci[None] * pi[:, :, :, None, :]
    cai = cr[None] * pi[:, :, :, None, :] + ci[None] * pr[:, :, :, None, :]
    kern = (jnp.einsum('tdgmp,dgpn->tdgnm', car, bbr, precision=HIGHEST)
            - jnp.einsum('tdgmp,dgpn->tdgnm', cai, bbi, precision=HIGHEST))
    s_in = jnp.arange(cs)[:, None]
    s_out = jnp.arange(cs)[None, :]
    lag = s_out - s_in
    kf = kern[jnp.clip(lag, 0, cs), 0]
    kb = kern[jnp.clip(-lag, 0, cs), 1]
    tt = (jnp.where((lag >= 0)[:, :, None, None, None], kf, 0.0)
          + jnp.where((lag <= 0)[:, :, None, None, None], kb, 0.0))
    eye_s = jnp.eye(cs, dtype=F32)[:, :, None, None, None]
    eye_n = jnp.eye(n, dtype=F32)[None, None, None]
    tt = tt + eye_s * eye_n * d_skip.astype(F32).reshape(g, n)[None, None, :, :, None]
    t_g = jnp.transpose(tt, (2, 0, 3, 1, 4)).reshape(g, cs * n, cs * n)
    pf_r, pf_i = pr[cs - 1 - jnp.arange(cs), 0], pi[cs - 1 - jnp.arange(cs), 0]
    pb_r, pb_i = pr[jnp.arange(cs), 1], pi[jnp.arange(cs), 1]

    def cmul_b(qr, qi, d):
        re = qr[:, :, :, None] * bbr[d][None] - qi[:, :, :, None] * bbi[d][None]
        im = qr[:, :, :, None] * bbi[d][None] + qi[:, :, :, None] * bbr[d][None]
        return jnp.transpose(re, (1, 0, 3, 2)), jnp.transpose(im, (1, 0, 3, 2))

    ffr, ffi = cmul_b(pf_r, pf_i, 0)
    fbr, fbi = cmul_b(pb_r, pb_i, 1)
    f_g = jnp.stack([ffr, ffi, fbr, fbi], axis=3).reshape(g, cs * n, 4, p)
    ef_r, ef_i = car[1 + jnp.arange(cs), 0], cai[1 + jnp.arange(cs), 0]
    eb_r, eb_i = car[cs - jnp.arange(cs), 1], cai[cs - jnp.arange(cs), 1]
    e_g = jnp.stack([ef_r, -ef_i, eb_r, -eb_i], axis=0)
    e_g = jnp.transpose(e_g, (2, 0, 4, 1, 3)).reshape(g, 4, p, cs * n)
    eye2 = jnp.eye(2, dtype=F32)
    w = cs * n
    t_p = (t_g.reshape(g // 2, 2, w, 1, w) * eye2[None, :, None, :, None]).reshape(g // 2, 2 * w, 2 * w)
    f_p = (f_g.reshape(g // 2, 2, w, 4, 1, p) * eye2[None, :, None, None, :, None]).reshape(g // 2, 2 * w, 8 * p)
    e_p = (jnp.transpose(e_g.reshape(g // 2, 2, 4, p, w), (0, 2, 1, 3, 4))[:, :, :, :, None, :]
           * eye2[None, None, :, None, :, None]).reshape(g // 2, 8 * p, 2 * w)
    a_p = jnp.stack([pr[cs, 0], pi[cs, 0], pr[cs, 1], pi[cs, 1]], axis=1).reshape(g // 2, 2, 4, p)
    a_p = jnp.transpose(a_p, (0, 2, 1, 3)).reshape(g // 2, 4, 2 * p)
    return t_p.astype(BF16), f_p.astype(BF16), e_p.astype(BF16), a_p


def _s5_to_chunks(zs_list, nc):
    cs, n, g = S5_CHUNK, S5_GROUP, S5_GROUPS
    full = nc * cs
    zs = jnp.concatenate([jnp.pad(a, ((0, 0), (0, full - a.shape[1]), (0, 0))) for a in zs_list], axis=0)
    nseq = zs.shape[0]
    u = zs.reshape(nseq, nc, cs, g, n)
    return jnp.transpose(u, (3, 1, 0, 2, 4)).reshape(g, nc * nseq, cs * n)


def _s5_from_chunks(y, nc, nseq):
    cs, n, g = S5_CHUNK, S5_GROUP, S5_GROUPS
    y = y.reshape(g, nc, nseq, cs, n)
    return jnp.transpose(y, (2, 1, 3, 0, 4)).reshape(nseq, nc * cs, g * n)


def _rope_tables(seq_len):
    half = ROT_DIM // 2
    inv = ROPE_THETA ** (-jnp.arange(0, ROT_DIM, 2, dtype=F32) / ROT_DIM)
    ang = jnp.arange(seq_len, dtype=F32)[:, None] * inv[None, :]
    cos, sin = jnp.cos(ang), jnp.sin(ang)
    ones = jnp.ones((seq_len, DA_SUB_DIM - ROT_DIM), F32)
    zeros = jnp.zeros((seq_len, DA_SUB_DIM - ROT_DIM), F32)
    zh = jnp.zeros((seq_len, half), F32)
    c = jnp.concatenate([cos, cos, ones], axis=1)
    s1 = jnp.concatenate([-sin, zh, zeros], axis=1)
    s2 = jnp.concatenate([zh, sin, zeros], axis=1)
    rep = DA_HEAD_DIM // DA_SUB_DIM
    return jnp.tile(c, (1, rep)), jnp.tile(s1, (1, rep)), jnp.tile(s2, (1, rep))


def kernel(x_prompt, x_sample, c_prompt, c_sample, w_ada, b_ada, norm_g, w_ff_gate, w_ff_up, w_ff_down, w_in, w_out, qk_norm_g, diff_lambda, diff_subln_g, hg_lb_logits, hg_gnorm_g, s5_lam_re, s5_lam_im, s5_log_dt, s5_b_re, s5_b_im, s5_c_re, s5_c_im, s5_d, w_glu, b_glu):
    depth, d = norm_g.shape[0], norm_g.shape[2]
    d_ff = w_ff_gate.shape[-1]
    trunks = []
    seq0 = 0
    for xin in (x_prompt, x_sample):
        b, l, _ = xin.shape
        trunks.append(dict(x=xin.reshape(b * l, d), b=b, l=l, seq0=seq0, rope=_rope_tables(l)))
        seq0 += b
    nseq = seq0
    c_all = jnp.concatenate([c_prompt, c_sample], axis=0)

    wg = w_ff_gate.astype(BF16).reshape(depth * 2, d, d_ff)
    wu = w_ff_up.astype(BF16).reshape(depth * 2, d, d_ff)
    wd = w_ff_down.astype(BF16).reshape(depth * 2, d_ff, d)
    w_in_b = w_in.astype(BF16)
    w_out_b = w_out.astype(BF16)
    w_glu_b = w_glu.astype(BF16)
    b_ada3 = b_ada.reshape(depth, 1, -1)
    b_glu3 = b_glu.reshape(depth, 1, -1)
    norm_g3 = norm_g.reshape(depth * 4, 1, d)

    lbs = jax.nn.softmax(hg_lb_logits.astype(F32), axis=1)
    lbs = jnp.cumsum(lbs, axis=1) - lbs[:, :1]
    max_l = max(tr["l"] for tr in trunks)
    nc_s5 = max_l // S5_CHUNK
    hg_col0 = 3 * DA_HEADS
    s5_col0 = (3 * DA_HEADS + 5 * HG_HEADS) * V7X_LANES

    for layer in range(depth):
        mod3 = _ada(c_all, w_ada, b_ada3, layer).reshape(nseq * N_MOD, 1, d)
        lam_init = 0.8 - 0.6 * math.exp(-0.3 * layer)
        lv = diff_lambda[layer].astype(F32)
        lam = jnp.exp(jnp.sum(lv[0] * lv[1])) - jnp.exp(jnp.sum(lv[2] * lv[3])) + lam_init
        lam_vec = jnp.full((1, V7X_LANES), lam, F32)
        subln = (diff_subln_g[layer].astype(F32) * (1.0 - lam_init)).reshape(1, DA_HEAD_DIM)
        rep = DA_HEAD_DIM // DA_SUB_DIM
        gq = jnp.tile(qk_norm_g[layer, 0].astype(F32), rep) * (DA_SUB_DIM ** -0.5)
        gk = jnp.tile(qk_norm_g[layer, 1].astype(F32), rep)
        gains = jnp.concatenate([jnp.tile(gq[None], (DA_HEADS, 1)), jnp.tile(gk[None], (DA_HEADS, 1)),
                                 jnp.ones((DA_HEADS, DA_HEAD_DIM), F32)], axis=0)[:, None, :]
        lb_f = lbs[0, layer].reshape(HG_HEADS, 1, HG_DIM)
        lb_b = lbs[1, layer].reshape(HG_HEADS, 1, HG_DIM)
        gn_g = hg_gnorm_g[layer].astype(F32).reshape(1, HG_DIM)
        s5_tabs = _s5_tables(s5_lam_re[layer], s5_lam_im[layer], s5_log_dt[layer], s5_b_re[layer],
                             s5_b_im[layer], s5_c_re[layer], s5_c_im[layer], s5_d[layer])

        zs_list = []
        for tr in trunks:
            b, l, s0 = tr["b"], tr["l"], tr["seq0"]
            x = tr["x"]
            h = _normmod(x, norm_g3, layer * 4 + 0, mod3, s0, l, 0, 1)
            act = _ffup(h, wg, wu, layer * 2 + 0)
            x = _ffdown(act, wd, layer * 2 + 0, x, mod3, s0, l, 2, 0.5)
            h = _normmod(x, norm_g3, layer * 4 + 1, mod3, s0, l, 3, 4)
            z = _mm_in(h, w_in_b, layer)
            qkv = _qkv_prep(z, gains, *tr["rope"], l)
            tr["o_da"] = _attention(qkv, lam_vec, subln, b, l)
            tr["o_hg"] = _hgrn(z, lb_f, lb_b, gn_g, b, l, hg_col0)
            zs_list.append(z[:, s5_col0:].astype(BF16).reshape(b, l, -1))
            tr["x"] = x

        y_all = _s5(_s5_to_chunks(zs_list, nc_s5), *s5_tabs, nc_s5, nseq)
        y_all = _s5_from_chunks(y_all, nc_s5, nseq)

        for tr in trunks:
            b, l, s0 = tr["b"], tr["l"], tr["seq0"]
            x = tr["x"]
            y = y_all[s0:s0 + b, :l].reshape(b * l, -1)
            o_s5 = _glu(y, w_glu_b, b_glu3, layer)
            x = _mixout(tr["o_da"], tr["o_hg"], o_s5, w_out_b, layer, x, mod3, s0, l, 5)
            h = _normmod(x, norm_g3, layer * 4 + 2, mod3, s0, l, 6, 7)
            act = _ffup(h, wg, wu, layer * 2 + 1)
            x = _ffdown(act, wd, layer * 2 + 1, x, mod3, s0, l, 8, 0.5)
            tr["x"] = _norm(x, norm_g3, layer * 4 + 3)

    return tuple(tr["x"].reshape(tr["b"], tr["l"], d) for tr in trunks)
```

```python
import functools
import math

import jax
import jax.numpy as jnp
import numpy as np
from jax import lax
from jax.experimental import pallas as pl
from jax.experimental.pallas import tpu as pltpu

F32 = jnp.float32
BF16 = jnp.bfloat16
HIGHEST = lax.Precision.HIGHEST

V7X_LANES = 128
V7X_SUBLANES = 8
V7X_VMEM_LIMIT = 56 * 1024 * 1024

NORM_EPS = 1e-6
N_MOD = 9
DA_HEADS = 8
DA_HEAD_DIM = 128
DA_SUB_DIM = 64
ROT_DIM = 16
ROPE_THETA = 500000.0
HG_HEADS = 8
HG_DIM = 128
LB_FLOOR = 1e-6
GLA_CHUNK = 128
GLA_DIAG = 8
S5_GROUP = 16
S5_GROUPS = 128
S5_STATE = 64
S5_CHUNK = 8
S5_ROW_TILE = 256


def _params(sem, vmem=V7X_VMEM_LIMIT):
    return pltpu.CompilerParams(dimension_semantics=sem, vmem_limit_bytes=vmem)


def _dot(a, b):
    return jnp.dot(a, b, preferred_element_type=F32)


def _dot_nt(a, b):
    return lax.dot_general(a, b, (((1,), (1,)), ((), ())), preferred_element_type=F32)


def _dot_tn(a, b):
    return lax.dot_general(a, b, (((0,), (0,)), ((), ())), preferred_element_type=F32)


def _sigmoid(x):
    return 1.0 / (1.0 + jnp.exp(-x))


def _ada_kernel(c_ref, w_ref, b_ref, o_ref):
    c = c_ref[...]
    a = c * _sigmoid(c)
    o_ref[...] = jnp.dot(a, w_ref[...], preferred_element_type=F32, precision=HIGHEST) + b_ref[...]


def _ada(c, w_ada, b_ada3, layer, tn=512):
    s, d = c.shape
    n = w_ada.shape[-1]
    return pl.pallas_call(
        _ada_kernel,
        grid=(n // tn,),
        in_specs=[
            pl.BlockSpec((s, d), lambda j: (0, 0)),
            pl.BlockSpec((None, d, tn), lambda j: (layer, 0, j)),
            pl.BlockSpec((None, 1, tn), lambda j: (layer, 0, j)),
        ],
        out_specs=pl.BlockSpec((s, tn), lambda j: (0, j)),
        out_shape=jax.ShapeDtypeStruct((s, n), F32),
        compiler_params=_params(("arbitrary",)),
        name="ada",
    )(c, w_ada, b_ada3)


def _rms(x):
    return x * lax.rsqrt(jnp.mean(x * x, axis=-1, keepdims=True) + NORM_EPS)


def _normmod_kernel(x_ref, g_ref, sh_ref, sc_ref, o_ref):
    y = _rms(x_ref[...]) * g_ref[...]
    o_ref[...] = (y * (1.0 + sc_ref[...]) + sh_ref[...]).astype(o_ref.dtype)


def _normmod(x, norm_g3, gi, mod3, seq0, rows_per_seq, j_shift, j_scale, tm=256):
    t, d = x.shape
    tm = min(tm, rows_per_seq)
    bps = rows_per_seq // tm
    return pl.pallas_call(
        _normmod_kernel,
        grid=(t // tm,),
        in_specs=[
            pl.BlockSpec((tm, d), lambda i: (i, 0)),
            pl.BlockSpec((None, 1, d), lambda i: (gi, 0, 0)),
            pl.BlockSpec((None, 1, d), lambda i: ((seq0 + i // bps) * N_MOD + j_shift, 0, 0)),
            pl.BlockSpec((None, 1, d), lambda i: ((seq0 + i // bps) * N_MOD + j_scale, 0, 0)),
        ],
        out_specs=pl.BlockSpec((tm, d), lambda i: (i, 0)),
        out_shape=jax.ShapeDtypeStruct((t, d), BF16),
        compiler_params=_params(("parallel",)),
        name="normmod",
    )(x, norm_g3, mod3, mod3)


def _norm_kernel(x_ref, g_ref, o_ref):
    o_ref[...] = _rms(x_ref[...]) * g_ref[...]


def _norm(x, norm_g3, gi, tm=256):
    t, d = x.shape
    tm = min(tm, t)
    return pl.pallas_call(
        _norm_kernel,
        grid=(t // tm,),
        in_specs=[
            pl.BlockSpec((tm, d), lambda i: (i, 0)),
            pl.BlockSpec((None, 1, d), lambda i: (gi, 0, 0)),
        ],
        out_specs=pl.BlockSpec((tm, d), lambda i: (i, 0)),
        out_shape=jax.ShapeDtypeStruct((t, d), F32),
        compiler_params=_params(("parallel",)),
        name="norm",
    )(x, norm_g3)


def _ffup_kernel(h_ref, wg_ref, wu_ref, o_ref):
    h = h_ref[...]
    g = _dot(h, wg_ref[...])
    u = _dot(h, wu_ref[...])
    o_ref[...] = (g * _sigmoid(g) * u).astype(o_ref.dtype)


def _ffup(h, wg, wu, li, tm=1024, tf=256):
    t, d = h.shape
    f = wg.shape[-1]
    tm, tf = min(tm, t), min(tf, f)
    return pl.pallas_call(
        _ffup_kernel,
        grid=(t // tm, f // tf),
        in_specs=[
            pl.BlockSpec((tm, d), lambda i, j: (i, 0)),
            pl.BlockSpec((None, d, tf), lambda i, j: (li, 0, j)),
            pl.BlockSpec((None, d, tf), lambda i, j: (li, 0, j)),
        ],
        out_specs=pl.BlockSpec((tm, tf), lambda i, j: (i, j)),
        out_shape=jax.ShapeDtypeStruct((t, f), BF16),
        compiler_params=_params(("parallel", "arbitrary")),
        name="ffup",
    )(h, wg, wu)


def _ffdown_kernel(a_ref, w_ref, x_ref, gate_ref, o_ref, *, scale):
    acc = _dot(a_ref[...], w_ref[...])
    o_ref[...] = x_ref[...] + (scale * gate_ref[...]) * acc


def _ffdown(act, wd, li, x, mod3, seq0, rows_per_seq, j_gate, scale, tm=512, tn=256):
    t, f = act.shape
    d = wd.shape[-1]
    tm, tn = min(tm, rows_per_seq), min(tn, d)
    bps = rows_per_seq // tm
    return pl.pallas_call(
        functools.partial(_ffdown_kernel, scale=scale),
        grid=(t // tm, d // tn),
        in_specs=[
            pl.BlockSpec((tm, f), lambda i, j: (i, 0)),
            pl.BlockSpec((None, f, tn), lambda i, j: (li, 0, j)),
            pl.BlockSpec((tm, tn), lambda i, j: (i, j)),
            pl.BlockSpec((None, 1, tn), lambda i, j: ((seq0 + i // bps) * N_MOD + j_gate, 0, j)),
        ],
        out_specs=pl.BlockSpec((tm, tn), lambda i, j: (i, j)),
        out_shape=jax.ShapeDtypeStruct((t, d), F32),
        compiler_params=_params(("parallel", "arbitrary")),
        name="ffdown",
    )(act, wd, x, mod3)


def _mm_kernel(a_ref, w_ref, o_ref):
    o_ref[...] = _dot(a_ref[...], w_ref[...]).astype(o_ref.dtype)


def _mm_in(h, w_in, layer, out_dtype, tm=1024, tn=512):
    t, d = h.shape
    n = w_in.shape[-1]
    tm, tn = min(tm, t), min(tn, n)
    return pl.pallas_call(
        _mm_kernel,
        grid=(t // tm, n // tn),
        in_specs=[
            pl.BlockSpec((tm, d), lambda i, j: (i, 0)),
            pl.BlockSpec((None, d, tn), lambda i, j: (layer, 0, j)),
        ],
        out_specs=pl.BlockSpec((tm, tn), lambda i, j: (i, j)),
        out_shape=jax.ShapeDtypeStruct((t, n), out_dtype),
        compiler_params=_params(("parallel", "arbitrary")),
        name="mm_in",
    )(h, w_in)


def _mixout_kernel(da_ref, hg_ref, s5_ref, w_ref, x_ref, gate_ref, o_ref, *, w_da, w_hg):
    acc = _dot(da_ref[...], w_ref[0:w_da, :])
    acc = acc + _dot(hg_ref[...], w_ref[w_da:w_da + w_hg, :])
    acc = acc + _dot(s5_ref[...], w_ref[w_da + w_hg:, :])
    o_ref[...] = x_ref[...] + gate_ref[...] * acc


def _mixout(o_da, o_hg, o_s5, w_out, layer, x, mod3, seq0, rows_per_seq, j_gate, tm=1024, tn=512):
    t, w_da = o_da.shape
    w_hg, w_s5 = o_hg.shape[1], o_s5.shape[1]
    d_in, d = w_out.shape[1], w_out.shape[2]
    tm, tn = min(tm, rows_per_seq), min(tn, d)
    bps = rows_per_seq // tm
    return pl.pallas_call(
        functools.partial(_mixout_kernel, w_da=w_da, w_hg=w_hg),
        grid=(t // tm, d // tn),
        in_specs=[
            pl.BlockSpec((tm, w_da), lambda i, j: (i, 0)),
            pl.BlockSpec((tm, w_hg), lambda i, j: (i, 0)),
            pl.BlockSpec((tm, w_s5), lambda i, j: (i, 0)),
            pl.BlockSpec((None, d_in, tn), lambda i, j: (layer, 0, j)),
            pl.BlockSpec((tm, tn), lambda i, j: (i, j)),
            pl.BlockSpec((None, 1, tn), lambda i, j: ((seq0 + i // bps) * N_MOD + j_gate, 0, j)),
        ],
        out_specs=pl.BlockSpec((tm, tn), lambda i, j: (i, j)),
        out_shape=jax.ShapeDtypeStruct((t, d), F32),
        compiler_params=_params(("parallel", "arbitrary")),
        name="mixout",
    )(o_da, o_hg, o_s5, w_out, x, mod3)


def _glu_kernel(y_ref, w_ref, b_ref, yt_ref, o_ref):
    acc = _dot(y_ref[...], w_ref[...]) + b_ref[...]
    o_ref[...] = (yt_ref[...].astype(F32) * _sigmoid(acc)).astype(o_ref.dtype)


def _glu(y, w_glu, b_glu3, layer, tm=1024, tn=512):
    t, d = y.shape
    tm, tn = min(tm, t), min(tn, d)
    return pl.pallas_call(
        _glu_kernel,
        grid=(t // tm, d // tn),
        in_specs=[
            pl.BlockSpec((tm, d), lambda i, j: (i, 0)),
            pl.BlockSpec((None, d, tn), lambda i, j: (layer, 0, j)),
            pl.BlockSpec((None, 1, tn), lambda i, j: (layer, 0, j)),
            pl.BlockSpec((tm, tn), lambda i, j: (i, j)),
        ],
        out_specs=pl.BlockSpec((tm, tn), lambda i, j: (i, j)),
        out_shape=jax.ShapeDtypeStruct((t, d), BF16),
        compiler_params=_params(("parallel", "arbitrary")),
        name="glu",
    )(y, w_glu, b_glu3, y)


def _qk_kernel(z_ref, g_ref, c_ref, s1_ref, s2_ref, o_ref):
    c, s1, s2, g = c_ref[...], s1_ref[...], s2_ref[...], g_ref[...]
    lane = lax.broadcasted_iota(jnp.int32, c.shape, 1)
    lo = lane < DA_SUB_DIM
    half = ROT_DIM // 2
    for h in range(DA_HEADS):
        cols = slice(h * V7X_LANES, (h + 1) * V7X_LANES)
        x = z_ref[:, cols].astype(F32)
        x2 = x * x
        s_lo = jnp.sum(jnp.where(lo, x2, 0.0), axis=-1, keepdims=True)
        s_hi = jnp.sum(jnp.where(lo, 0.0, x2), axis=-1, keepdims=True)
        ms = jnp.where(lo, s_lo, s_hi) * (1.0 / DA_SUB_DIM)
        y = x * lax.rsqrt(ms + NORM_EPS) * g
        y_next = pltpu.roll(y, V7X_LANES - half, axis=1)
        y_prev = pltpu.roll(y, half, axis=1)
        o_ref[:, cols] = (y * c + y_next * s1 + y_prev * s2).astype(o_ref.dtype)


def _qk_prep(za, gains, rope_c, rope_s1, rope_s2, seq_len, tm=512):
    t = za.shape[0]
    tm = min(tm, seq_len)
    bps = seq_len // tm
    w = DA_HEADS * DA_HEAD_DIM
    tab = pl.BlockSpec((tm, V7X_LANES), lambda i, j: (i % bps, 0))
    return pl.pallas_call(
        _qk_kernel,
        grid=(t // tm, 2),
        in_specs=[
            pl.BlockSpec((tm, w), lambda i, j: (i, j)),
            pl.BlockSpec((None, 1, V7X_LANES), lambda i, j: (j, 0, 0)),
            tab, tab, tab,
        ],
        out_specs=pl.BlockSpec((tm, w), lambda i, j: (i, j)),
        out_shape=jax.ShapeDtypeStruct((t, 2 * w), BF16),
        compiler_params=_params(("parallel", "arbitrary")),
        name="qk_prep",
    )(za, gains, rope_c, rope_s1, rope_s2)


def _attn_kernel(q_ref, k_ref, v_ref, lam_ref, g_ref, o_ref, s_ref, m_ref, l_ref, acc_ref, *, tk, nk):
    q = q_ref[...]
    tq = q.shape[0]
    w = V7X_LANES
    nl = tk // w
    lane = lax.broadcasted_iota(jnp.int32, q.shape, 1)
    lo = lane < DA_SUB_DIM
    zero = jnp.zeros_like(q)
    q2 = jnp.concatenate([jnp.where(lo, q, zero), jnp.where(lo, zero, q)], axis=0)

    m_ref[...] = jnp.full(m_ref.shape, -jnp.inf, F32)

    def pass1(j, carry):
        rows = pl.ds(pl.multiple_of(j * tk, tk), tk)
        s = _dot_nt(q2, k_ref[rows, :])
        s_ref[j] = s
        mx = s[:, 0:w]
        for a in range(1, nl):
            mx = jnp.maximum(mx, s[:, a * w:(a + 1) * w])
        m_ref[...] = jnp.maximum(m_ref[...], mx)
        return carry

    unroll = 2 if nk % 2 == 0 else 1
    lax.fori_loop(0, nk, pass1, 0, unroll=unroll)
    m_ref[...] = jnp.broadcast_to(jnp.max(m_ref[...], axis=-1, keepdims=True), m_ref.shape)

    l_ref[...] = jnp.zeros(l_ref.shape, F32)
    acc_ref[...] = jnp.zeros(acc_ref.shape, F32)

    def pass2(j, carry):
        rows = pl.ds(pl.multiple_of(j * tk, tk), tk)
        s = s_ref[j]
        mb = m_ref[...]
        ps = [jnp.exp(s[:, a * w:(a + 1) * w] - mb) for a in range(nl)]
        lsum = ps[0]
        for a in range(1, nl):
            lsum = lsum + ps[a]
        l_ref[...] = l_ref[...] + lsum
        p = jnp.concatenate([x.astype(BF16) for x in ps], axis=1)
        acc_ref[...] = acc_ref[...] + _dot(p, v_ref[rows, :])
        return carry

    lax.fori_loop(0, nk, pass2, 0, unroll=unroll)
    acc = acc_ref[...]
    l = jnp.sum(l_ref[...], axis=-1, keepdims=True)
    o = acc[:tq] / l[:tq] - lam_ref[...] * (acc[tq:] / l[tq:])
    o_ref[...] = (_rms(o) * g_ref[...]).astype(o_ref.dtype)


def _attention(qk, za, lam_vec, subln_g, batch, seq_len, v_blk0, tq=256, tk=512):
    t = qk.shape[0]
    tq, tk = min(tq, seq_len), min(tk, seq_len)
    nq, nk = seq_len // tq, seq_len // tk
    return pl.pallas_call(
        functools.partial(_attn_kernel, tk=tk, nk=nk),
        grid=(batch, DA_HEADS, nq),
        in_specs=[
            pl.BlockSpec((tq, V7X_LANES), lambda b, h, i: (b * nq + i, h)),
            pl.BlockSpec((seq_len, V7X_LANES), lambda b, h, i: (b, DA_HEADS + h)),
            pl.BlockSpec((seq_len, V7X_LANES), lambda b, h, i: (b, v_blk0 + h)),
            pl.BlockSpec((1, V7X_LANES), lambda b, h, i: (0, 0)),
            pl.BlockSpec((1, V7X_LANES), lambda b, h, i: (0, 0)),
        ],
        out_specs=pl.BlockSpec((tq, V7X_LANES), lambda b, h, i: (b * nq + i, h)),
        out_shape=jax.ShapeDtypeStruct((t, DA_HEADS * DA_HEAD_DIM), BF16),
        scratch_shapes=[
            pltpu.VMEM((nk, 2 * tq, tk), F32),
            pltpu.VMEM((2 * tq, V7X_LANES), F32),
            pltpu.VMEM((2 * tq, V7X_LANES), F32),
            pltpu.VMEM((2 * tq, V7X_LANES), F32),
        ],
        compiler_params=_params(("parallel", "parallel", "arbitrary")),
        name="diff_attn",
    )(qk, qk, za, lam_vec, subln_g)


def _bcast_rows(cum, idxs, rows_each):
    parts = [jnp.broadcast_to(cum[i:i + 1, :], (rows_each, cum.shape[1])) for i in idxs]
    return parts[0] if len(parts) == 1 else jnp.concatenate(parts, axis=0)


def _split3(x):
    hi = x.astype(BF16)
    r1 = x - hi.astype(F32)
    mid = r1.astype(BF16)
    lo = (r1 - mid.astype(F32)).astype(BF16)
    return hi, mid, lo


def _gla_masks(c, rev):
    t = np.arange(c)[:, None]
    s = np.arange(c)[None, :]
    vis = (s >= t) if rev else (s <= t)
    level = np.full((c, c), -1, np.int32)
    blk, k = c, int(math.log2(c // GLA_DIAG))
    while blk >= GLA_DIAG:
        level = np.where(vis & (t // blk == s // blk), k, level)
        blk //= 2
        k -= 1
    return jnp.asarray(vis, BF16), jnp.asarray(level, jnp.int32)


def _gla_chunk(q, k, v, lf, st, rev, trib, level):
    c = q.shape[0]
    rowf = lax.broadcasted_iota(jnp.int32, q.shape, 0)
    hi, mid, lo = _split3(lf)
    cum = (_dot(trib, hi) + _dot(trib, mid)) + _dot(trib, lo)
    edge = cum[0:1, :] if rev else cum[c - 1:c, :]

    q_in = (q * jnp.exp(cum)).astype(BF16)
    k_st = (k * jnp.exp(edge - cum)).astype(BF16)
    vb = v.astype(BF16)
    o = _dot_nt(q_in, st.astype(BF16))
    st_new = st * jnp.exp(edge) + _dot_tn(vb, k_st)

    d = GLA_DIAG
    idxs = [j * d + (d // 2 if rev else d // 2 - 1) for j in range(c // d)]
    ref = _bcast_rows(cum, idxs, d)
    a = _dot_nt((q * jnp.exp(cum - ref)).astype(BF16), (k * jnp.exp(ref - cum)).astype(BF16))
    att = jnp.where(level == 0, a, 0.0)
    m, lev = d, 1
    while m < c:
        blk = 2 * m
        idxs = [j * blk + (m if rev else m - 1) for j in range(c // blk)]
        bnd = _bcast_rows(cum, idxs, blk)
        late = (rowf & (blk - 1)) >= m
        q_side = jnp.logical_not(late) if rev else late
        e = jnp.exp(jnp.where(q_side, cum - bnd, bnd - cum))
        a = _dot_nt((q * e).astype(BF16), (k * e).astype(BF16))
        att = jnp.where(level == lev, a, att)
        m, lev = blk, lev + 1
    o = o + _dot(att.astype(BF16), vb)
    return o, st_new


def _hgrn_kernel(qh_ref, ih_ref, gh_ref, zf_ref, zb_ref, lbf_ref, lbb_ref, gn_ref,
                 trif_ref, trib_ref, lvf_ref, lvb_ref, o_ref, of_ref, ob_ref, st_ref, *, chunk, nc, heads, ft):
    w = V7X_LANES

    def gates(z, lb):
        lbc = jnp.clip(lb, 0.0, 1.0 - LB_FLOOR)
        e = jnp.exp(-jnp.abs(z))
        r = 1.0 / (1.0 + e)
        pos = z >= 0.0
        sig = jnp.where(pos, r, e * r)
        nsig = jnp.where(pos, e * r, r)
        logf = jnp.minimum(jnp.log(jnp.maximum(lbc, LB_FLOOR) + (1.0 - lbc) * sig), 0.0)
        return logf, (1.0 - lbc) * nsig

    def load(rows, cols, z_ref, lb_ref):
        x = qh_ref[rows, cols].astype(F32)
        q = x * _sigmoid(x) * (HG_DIM ** -0.5)
        lf, k = gates(z_ref[rows, cols], lb_ref[:, cols])
        return q, k, ih_ref[rows, cols].astype(F32), lf

    st_ref[...] = jnp.zeros(st_ref.shape, F32)

    def step(i, carry):
        rf = pl.ds(pl.multiple_of(i * chunk, chunk), chunk)
        rb = pl.ds(pl.multiple_of((nc - 1 - i) * chunk, chunk), chunk)
        for hh in range(heads):
            cols = slice(hh * w, (hh + 1) * w)
            q, k, v, lf = load(rf, cols, zf_ref, lbf_ref)
            o, st = _gla_chunk(q, k, v, lf, st_ref[2 * hh], False, trif_ref[...], lvf_ref[...])
            st_ref[2 * hh] = st
            of_ref[rf, cols] = o
            q, k, v, lf = load(rb, cols, zb_ref, lbb_ref)
            o, st = _gla_chunk(q, k, v, lf, st_ref[2 * hh + 1], True, trib_ref[...], lvb_ref[...])
            st_ref[2 * hh + 1] = st
            ob_ref[rb, cols] = o
        return carry

    lax.fori_loop(0, nc, step, 0)

    def finish(i, carry):
        rows = pl.ds(pl.multiple_of(i * ft, ft), ft)
        for hh in range(heads):
            cols = slice(hh * w, (hh + 1) * w)
            o = of_ref[rows, cols] + ob_ref[rows, cols]
            g = gh_ref[rows, cols].astype(F32)
            o_ref[rows, cols] = (_rms(o) * gn_ref[...] * (g * _sigmoid(g))).astype(o_ref.dtype)
        return carry

    lax.fori_loop(0, (nc * chunk) // ft, finish, 0)


def _hgrn(za, zg, lb_f, lb_b, gn_g, batch, seq_len, q_blk0, i_blk0, g_blk0, chunk=GLA_CHUNK, heads=2):
    t = za.shape[0]
    chunk = min(chunk, seq_len)
    ft = min(512, seq_len)
    hw = heads * V7X_LANES
    npair = HG_HEADS // heads

    def spec(blk0):
        return pl.BlockSpec((seq_len, hw), lambda b, p: (b, blk0 // heads + p))

    lbspec = pl.BlockSpec((None, 1, hw), lambda b, p: (p, 0, 0))
    cc = pl.BlockSpec((chunk, chunk), lambda b, p: (0, 0))
    tri_f, lv_f = _gla_masks(chunk, False)
    tri_b, lv_b = _gla_masks(chunk, True)
    return pl.pallas_call(
        functools.partial(_hgrn_kernel, chunk=chunk, nc=seq_len // chunk, heads=heads, ft=ft),
        grid=(batch, npair),
        in_specs=[spec(q_blk0), spec(i_blk0), spec(g_blk0), spec(0), spec(HG_HEADS), lbspec, lbspec,
                  pl.BlockSpec((1, V7X_LANES), lambda b, p: (0, 0)), cc, cc, cc, cc],
        out_specs=pl.BlockSpec((seq_len, hw), lambda b, p: (b, p)),
        out_shape=jax.ShapeDtypeStruct((t, HG_HEADS * HG_DIM), BF16),
        scratch_shapes=[pltpu.VMEM((seq_len, hw), F32), pltpu.VMEM((seq_len, hw), F32),
                        pltpu.VMEM((2 * heads, HG_DIM, HG_DIM), F32)],
        compiler_params=_params(("parallel", "parallel")),
        name="hgrn2",
    )(za, za, za, zg, zg, lb_f, lb_b, gn_g, tri_f, tri_b, lv_f, lv_b)


def _gelu_tanh(x):
    return 0.5 * x * (1.0 + jnp.tanh(0.7978845608028654 * (x + 0.044715 * (x * x * x))))


def _s5_kernel(u_ref, t_ref, f_ref, e_ref, a_ref, y_ref, fs_ref, xs_ref, *, nc, nseq, tr):
    m = nc * nseq
    for r0 in range(0, m, tr):
        fs_ref[r0:r0 + tr, :] = _dot(u_ref[r0:r0 + tr, :], f_ref[...])

    w = V7X_LANES
    a = a_ref[...]
    afr, afi, abr, abi = (jnp.broadcast_to(a[i:i + 1, :], (nseq, w)) for i in range(4))

    def body(i, carry):
        xfr, xfi, xbr, xbi = carry
        rf = pl.ds(pl.multiple_of(i * nseq, nseq), nseq)
        rb = pl.ds(pl.multiple_of((nc - 1 - i) * nseq, nseq), nseq)
        xs_ref[rf, 0:w] = xfr
        xs_ref[rf, w:2 * w] = xfi
        xs_ref[rb, 2 * w:3 * w] = xbr
        xs_ref[rb, 3 * w:4 * w] = xbi
        nfr = afr * xfr - afi * xfi + fs_ref[rf, 0:w]
        nfi = afr * xfi + afi * xfr + fs_ref[rf, w:2 * w]
        nbr = abr * xbr - abi * xbi + fs_ref[rb, 2 * w:3 * w]
        nbi = abr * xbi + abi * xbr + fs_ref[rb, 3 * w:4 * w]
        return nfr, nfi, nbr, nbi

    z = jnp.zeros((nseq, w), F32)
    lax.fori_loop(0, nc, body, (z, z, z, z))

    for r0 in range(0, m, tr):
        y = _dot(u_ref[r0:r0 + tr, :], t_ref[...]) + _dot(xs_ref[r0:r0 + tr, :].astype(BF16), e_ref[...])
        y_ref[r0:r0 + tr, :] = _gelu_tanh(y).astype(y_ref.dtype)


def _s5(u, t_tab, f_tab, e_tab, a_tab, nc, nseq, tr=512):
    noct, m, ow = u.shape
    npair = t_tab.shape[0]
    pw = t_tab.shape[1]
    ppo = ow // pw
    sw = f_tab.shape[2]
    tr = min(tr, m)
    uspec = pl.BlockSpec((None, m, pw), lambda i: (i // ppo, 0, i % ppo))
    return pl.pallas_call(
        functools.partial(_s5_kernel, nc=nc, nseq=nseq, tr=tr),
        grid=(npair,),
        in_specs=[
            uspec,
            pl.BlockSpec((None, pw, pw), lambda i: (i, 0, 0)),
            pl.BlockSpec((None, pw, sw), lambda i: (i, 0, 0)),
            pl.BlockSpec((None, sw, pw), lambda i: (i, 0, 0)),
            pl.BlockSpec((None, 4, V7X_LANES), lambda i: (i, 0, 0)),
        ],
        out_specs=uspec,
        out_shape=jax.ShapeDtypeStruct(u.shape, BF16),
        scratch_shapes=[pltpu.VMEM((m, sw), F32), pltpu.VMEM((m, sw), F32)],
        compiler_params=_params(("parallel",)),
        name="s5",
    )(u, t_tab, f_tab, e_tab, a_tab)


def _s5_in_kernel(x_ref, perm_ref, o_ref, *, cs, noct, b0, nt0, nt1):
    b, j = pl.program_id(0), pl.program_id(1)
    valid = j < jnp.where(b < b0, nt0, nt1)
    w = V7X_LANES
    full = noct * w

    @pl.when(valid)
    def _():
        for o in range(noct):
            xc = jnp.concatenate([x_ref[:, s * full + o * w:s * full + (o + 1) * w] for s in range(cs)], axis=1)
            o_ref[o] = _dot(xc, perm_ref[...]).astype(o_ref.dtype)

    @pl.when(jnp.logical_not(valid))
    def _():
        o_ref[...] = jnp.zeros(o_ref.shape, o_ref.dtype)


def _s5_out_kernel(y_ref, perm_ref, o_ref, *, cs, noct):
    w = V7X_LANES
    full = noct * w
    for o in range(noct):
        yc = _dot_nt(y_ref[o], perm_ref[...]).astype(o_ref.dtype)
        for s in range(cs):
            o_ref[:, s * full + o * w:s * full + (o + 1) * w] = yc[:, s * w:(s + 1) * w]


def _s5_block_maps(seqs, tr):
    (b0, n0), (b1, n1) = seqs
    nt0, nt1 = n0 // tr, n1 // tr

    def tiles(b):
        return jnp.where(b < b0, nt0, nt1)

    def row_block(b, j):
        return jnp.where(b < b0, b * nt0 + j, b0 * nt0 + (b - b0) * nt1 + j)

    return b0, nt0, nt1, tiles, row_block


def _s5_in(zs, perm, seqs, cs, tr):
    b0, nt0, nt1, tiles, row_block = _s5_block_maps(seqs, tr)
    nseq, ntm = seqs[0][0] + seqs[1][0], max(nt0, nt1)
    noct = zs.shape[1] // (cs * V7X_LANES)
    ow = cs * V7X_LANES
    return pl.pallas_call(
        functools.partial(_s5_in_kernel, cs=cs, noct=noct, b0=b0, nt0=nt0, nt1=nt1),
        grid=(nseq, ntm),
        in_specs=[
            pl.BlockSpec((tr, zs.shape[1]), lambda b, j: (row_block(b, jnp.minimum(j, tiles(b) - 1)), 0)),
            pl.BlockSpec((ow, ow), lambda b, j: (0, 0)),
        ],
        out_specs=pl.BlockSpec((noct, tr, ow), lambda b, j: (0, j, b)),
        out_shape=jax.ShapeDtypeStruct((noct, ntm * tr, nseq * ow), BF16),
        compiler_params=_params(("parallel", "arbitrary")),
        name="s5_in",
    )(zs, perm)


def _s5_out(y, perm, seqs, cs, tr):
    _, nt0, nt1, _, _ = _s5_block_maps(seqs, tr)
    nseq, ntm = seqs[0][0] + seqs[1][0], max(nt0, nt1)
    noct = y.shape[0]
    ow = cs * V7X_LANES
    return pl.pallas_call(
        functools.partial(_s5_out_kernel, cs=cs, noct=noct),
        grid=(nseq, ntm),
        in_specs=[
            pl.BlockSpec((noct, tr, ow), lambda b, j: (0, j, b)),
            pl.BlockSpec((ow, ow), lambda b, j: (0, 0)),
        ],
        out_specs=pl.BlockSpec((tr, noct * ow), lambda b, j: (b * ntm + j, 0)),
        out_shape=jax.ShapeDtypeStruct((nseq * ntm * tr, noct * ow), BF16),
        compiler_params=_params(("parallel", "arbitrary")),
        name="s5_out",
    )(y, perm)


def _s5_perm(cs):
    src = np.arange(cs * V7X_LANES)
    s, g, n = src // V7X_LANES, (src % V7X_LANES) // S5_GROUP, src % S5_GROUP
    dst = g * (cs * S5_GROUP) + s * S5_GROUP + n
    p = np.zeros((cs * V7X_LANES, cs * V7X_LANES), np.float32)
    p[src, dst] = 1.0
    return jnp.asarray(p, BF16)


def _s5_tables(lam_re, lam_im, log_dt, b_re, b_im, c_re, c_im, d_skip):
    cs, n, p, g = S5_CHUNK, S5_GROUP, S5_STATE, S5_GROUPS
    lr = jnp.minimum(lam_re.astype(F32), -1e-4)
    li = lam_im.astype(F32)
    dt = jnp.exp(log_dt.astype(F32))[..., None]
    mag = jnp.exp(lr * dt)
    ar, ai = mag * jnp.cos(li * dt), mag * jnp.sin(li * dt)
    den = lr * lr + li * li
    er, ei = ar - 1.0, ai
    zr, zi = (er * lr + ei * li) / den, (ei * lr - er * li) / den
    br, bi = b_re.astype(F32), b_im.astype(F32)
    bbr = zr[..., None] * br - zi[..., None] * bi
    bbi = zr[..., None] * bi + zi[..., None] * br
    cr, ci = c_re.astype(F32), c_im.astype(F32)
    tau = jnp.arange(cs + 1, dtype=F32)[:, None, None, None]
    pmag = jnp.exp(lr[None] * dt[None] * tau)
    pr = pmag * jnp.cos(li[None] * dt[None] * tau)
    pi = pmag * jnp.sin(li[None] * dt[None] * tau)
    car = cr[None] * pr[:, :, :, None, :] - ci[None] * pi[:, :, :, None, :]
    cai = cr[None] * pi[:, :, :, None, :] + ci[None] * pr[:, :, :, None, :]
    kern = (jnp.einsum('tdgmp,dgpn->tdgnm', car, bbr, precision=HIGHEST)
            - jnp.einsum('tdgmp,dgpn->tdgnm', cai, bbi, precision=HIGHEST))
    s_in = jnp.arange(cs)[:, None]
    s_out = jnp.arange(cs)[None, :]
    lag = s_out - s_in
    kf = kern[jnp.clip(lag, 0, cs), 0]
    kb = kern[jnp.clip(-lag, 0, cs), 1]
    tt = (jnp.where((lag >= 0)[:, :, None, None, None], kf, 0.0)
          + jnp.where((lag <= 0)[:, :, None, None, None], kb, 0.0))
    eye_s = jnp.eye(cs, dtype=F32)[:, :, None, None, None]
    eye_n = jnp.eye(n, dtype=F32)[None, None, None]
    tt = tt + eye_s * eye_n * d_skip.astype(F32).reshape(g, n)[None, None, :, :, None]
    t_g = jnp.transpose(tt, (2, 0, 3, 1, 4)).reshape(g, cs * n, cs * n)
    pf_r, pf_i = pr[cs - 1 - jnp.arange(cs), 0], pi[cs - 1 - jnp.arange(cs), 0]
    pb_r, pb_i = pr[jnp.arange(cs), 1], pi[jnp.arange(cs), 1]

    def cmul_b(qr, qi, d):
        re = qr[:, :, :, None] * bbr[d][None] - qi[:, :, :, None] * bbi[d][None]
        im = qr[:, :, :, None] * bbi[d][None] + qi[:, :, :, None] * bbr[d][None]
        return jnp.transpose(re, (1, 0, 3, 2)), jnp.transpose(im, (1, 0, 3, 2))

    ffr, ffi = cmul_b(pf_r, pf_i, 0)
    fbr, fbi = cmul_b(pb_r, pb_i, 1)
    f_g = jnp.stack([ffr, ffi, fbr, fbi], axis=3).reshape(g, cs * n, 4, p)
    ef_r, ef_i = car[1 + jnp.arange(cs), 0], cai[1 + jnp.arange(cs), 0]
    eb_r, eb_i = car[cs - jnp.arange(cs), 1], cai[cs - jnp.arange(cs), 1]
    e_g = jnp.stack([ef_r, -ef_i, eb_r, -eb_i], axis=0)
    e_g = jnp.transpose(e_g, (2, 0, 4, 1, 3)).reshape(g, 4, p, cs * n)
    eye2 = jnp.eye(2, dtype=F32)
    w = cs * n
    t_p = (t_g.reshape(g // 2, 2, w, 1, w) * eye2[None, :, None, :, None]).reshape(g // 2, 2 * w, 2 * w)
    f_p = (f_g.reshape(g // 2, 2, w, 4, 1, p) * eye2[None, :, None, None, :, None]).reshape(g // 2, 2 * w, 8 * p)
    e_p = (jnp.transpose(e_g.reshape(g // 2, 2, 4, p, w), (0, 2, 1, 3, 4))[:, :, :, :, None, :]
           * eye2[None, None, :, None, :, None]).reshape(g // 2, 8 * p, 2 * w)
    a_p = jnp.stack([pr[cs, 0], pi[cs, 0], pr[cs, 1], pi[cs, 1]], axis=1).reshape(g // 2, 2, 4, p)
    a_p = jnp.transpose(a_p, (0, 2, 1, 3)).reshape(g // 2, 4, 2 * p)
    return t_p.astype(BF16), f_p.astype(BF16), e_p.astype(BF16), a_p


def _rope_tables(seq_len):
    half = ROT_DIM // 2
    inv = ROPE_THETA ** (-jnp.arange(0, ROT_DIM, 2, dtype=F32) / ROT_DIM)
    ang = jnp.arange(seq_len, dtype=F32)[:, None] * inv[None, :]
    cos, sin = jnp.cos(ang), jnp.sin(ang)
    ones = jnp.ones((seq_len, DA_SUB_DIM - ROT_DIM), F32)
    zeros = jnp.zeros((seq_len, DA_SUB_DIM - ROT_DIM), F32)
    zh = jnp.zeros((seq_len, half), F32)
    c = jnp.concatenate([cos, cos, ones], axis=1)
    s1 = jnp.concatenate([-sin, zh, zeros], axis=1)
    s2 = jnp.concatenate([zh, sin, zeros], axis=1)
    rep = DA_HEAD_DIM // DA_SUB_DIM
    return jnp.tile(c, (1, rep)), jnp.tile(s1, (1, rep)), jnp.tile(s2, (1, rep))


def kernel(x_prompt, x_sample, c_prompt, c_sample, w_ada, b_ada, norm_g, w_ff_gate, w_ff_up, w_ff_down, w_in, w_out, qk_norm_g, diff_lambda, diff_subln_g, hg_lb_logits, hg_gnorm_g, s5_lam_re, s5_lam_im, s5_log_dt, s5_b_re, s5_b_im, s5_c_re, s5_c_im, s5_d, w_glu, b_glu):
    depth, d = norm_g.shape[0], norm_g.shape[2]
    d_ff = w_ff_gate.shape[-1]
    trunks = []
    seq0 = 0
    for xin in (x_prompt, x_sample):
        b, l, _ = xin.shape
        trunks.append(dict(x=xin.reshape(b * l, d), b=b, l=l, seq0=seq0, rope=_rope_tables(l)))
        seq0 += b
    nseq = seq0
    c_all = jnp.concatenate([c_prompt, c_sample], axis=0)

    wg = w_ff_gate.astype(BF16).reshape(depth * 2, d, d_ff)
    wu = w_ff_up.astype(BF16).reshape(depth * 2, d, d_ff)
    wd = w_ff_down.astype(BF16).reshape(depth * 2, d_ff, d)
    n_da, n_hg = 3 * DA_HEADS * DA_HEAD_DIM, HG_HEADS * HG_DIM
    g0, g1 = n_da + n_hg, n_da + 3 * n_hg
    w_in_a = jnp.concatenate([w_in[..., :g0], w_in[..., g1:]], axis=-1).astype(BF16)
    w_in_g = w_in[..., g0:g1].astype(BF16)
    w_out_b = w_out.astype(BF16)
    w_glu_b = w_glu.astype(BF16)
    b_ada3 = b_ada.reshape(depth, 1, -1)
    b_glu3 = b_glu.reshape(depth, 1, -1)
    norm_g3 = norm_g.reshape(depth * 4, 1, d)

    lbs = jax.nn.softmax(hg_lb_logits.astype(F32), axis=1)
    lbs = jnp.cumsum(lbs, axis=1) - lbs[:, :1]
    s5_seqs = tuple((tr["b"], tr["l"] // S5_CHUNK) for tr in trunks)
    nc_s5 = max(n for _, n in s5_seqs)
    s5_tr = min(S5_ROW_TILE, min(n for _, n in s5_seqs))
    s5_perm = _s5_perm(S5_CHUNK)
    v_blk0 = 2 * DA_HEADS
    hq_blk0, hi_blk0, hgt_blk0 = 3 * DA_HEADS, 3 * DA_HEADS + HG_HEADS, 3 * DA_HEADS + 2 * HG_HEADS
    s5_col0 = (3 * DA_HEADS + 3 * HG_HEADS) * V7X_LANES
    hg_pair = 2

    for layer in range(depth):
        mod3 = _ada(c_all, w_ada, b_ada3, layer).reshape(nseq * N_MOD, 1, d)
        lam_init = 0.8 - 0.6 * math.exp(-0.3 * layer)
        lv = diff_lambda[layer].astype(F32)
        lam = jnp.exp(jnp.sum(lv[0] * lv[1])) - jnp.exp(jnp.sum(lv[2] * lv[3])) + lam_init
        lam_vec = jnp.full((1, V7X_LANES), lam, F32)
        subln = (diff_subln_g[layer].astype(F32) * (1.0 - lam_init)).reshape(1, DA_HEAD_DIM)
        rep = DA_HEAD_DIM // DA_SUB_DIM
        gq = jnp.tile(qk_norm_g[layer, 0].astype(F32), rep) * (DA_SUB_DIM ** -0.5)
        gk = jnp.tile(qk_norm_g[layer, 1].astype(F32), rep)
        gains = jnp.stack([gq, gk], axis=0)[:, None, :]
        lb_f = lbs[0, layer].reshape(HG_HEADS // hg_pair, 1, hg_pair * HG_DIM)
        lb_b = lbs[1, layer].reshape(HG_HEADS // hg_pair, 1, hg_pair * HG_DIM)
        gn_g = hg_gnorm_g[layer].astype(F32).reshape(1, HG_DIM)
        s5_tabs = _s5_tables(s5_lam_re[layer], s5_lam_im[layer], s5_log_dt[layer], s5_b_re[layer],
                             s5_b_im[layer], s5_c_re[layer], s5_c_im[layer], s5_d[layer])

        zs_list = []
        for tr in trunks:
            b, l, s0 = tr["b"], tr["l"], tr["seq0"]
            x = tr["x"]
            h = _normmod(x, norm_g3, layer * 4 + 0, mod3, s0, l, 0, 1)
            act = _ffup(h, wg, wu, layer * 2 + 0)
            x = _ffdown(act, wd, layer * 2 + 0, x, mod3, s0, l, 2, 0.5)
            h = _normmod(x, norm_g3, layer * 4 + 1, mod3, s0, l, 3, 4)
            za = _mm_in(h, w_in_a, layer, BF16)
            zg = _mm_in(h, w_in_g, layer, F32)
            qk = _qk_prep(za, gains, *tr["rope"], l)
            tr["o_da"] = _attention(qk, za, lam_vec, subln, b, l, v_blk0)
            tr["o_hg"] = _hgrn(za, zg, lb_f, lb_b, gn_g, b, l, hq_blk0, hi_blk0, hgt_blk0, heads=hg_pair)
            zs_list.append(za[:, s5_col0:])
            tr["x"] = x

        zs = jnp.concatenate(zs_list, axis=0)
        s5_w = zs.shape[1]
        u = _s5_in(zs.reshape(-1, S5_CHUNK * s5_w), s5_perm, s5_seqs, S5_CHUNK, s5_tr)
        noct = u.shape[0]
        y_all = _s5(u.reshape(noct, nc_s5 * nseq, -1), *s5_tabs, nc_s5, nseq)
        y_all = _s5_out(y_all.reshape(u.shape), s5_perm, s5_seqs, S5_CHUNK, s5_tr)
        y_all = y_all.reshape(nseq, nc_s5 * S5_CHUNK, s5_w)

        for tr in trunks:
            b, l, s0 = tr["b"], tr["l"], tr["seq0"]
            x = tr["x"]
            y = y_all[s0:s0 + b, :l].reshape(b * l, s5_w)
            o_s5 = _glu(y, w_glu_b, b_glu3, layer)
            x = _mixout(tr["o_da"], tr["o_hg"], o_s5, w_out_b, layer, x, mod3, s0, l, 5)
            h = _normmod(x, norm_g3, layer * 4 + 2, mod3, s0, l, 6, 7)
            act = _ffup(h, wg, wu, layer * 2 + 1)
            x = _ffdown(act, wd, layer * 2 + 1, x, mod3, s0, l, 8, 0.5)
            tr["x"] = _norm(x, norm_g3, layer * 4 + 3)

    return tuple(tr["x"].reshape(tr["b"], tr["l"], d) for tr in trunks)
```

```python
import functools
import math

import jax
import jax.numpy as jnp
import numpy as np
from jax import lax
from jax.experimental import pallas as pl
from jax.experimental.pallas import tpu as pltpu

F32 = jnp.float32
BF16 = jnp.bfloat16
HIGHEST = lax.Precision.HIGHEST

V7X_LANES = 128
V7X_SUBLANES = 8
V7X_VMEM_LIMIT = 56 * 1024 * 1024

NORM_EPS = 1e-6
N_MOD = 9
DA_HEADS = 8
DA_HEAD_DIM = 128
DA_SUB_DIM = 64
ROT_DIM = 16
ROPE_THETA = 500000.0
HG_HEADS = 8
HG_DIM = 128
LB_FLOOR = 1e-6
GLA_CHUNK = 128
GLA_DIAG = 8
S5_GROUP = 16
S5_GROUPS = 128
S5_STATE = 64
S5_CHUNK = 8
S5_ROW_TILE = 256


def _params(sem, vmem=V7X_VMEM_LIMIT):
    return pltpu.CompilerParams(dimension_semantics=sem, vmem_limit_bytes=vmem)


def _dot(a, b):
    return jnp.dot(a, b, preferred_element_type=F32)


def _dot_nt(a, b):
    return lax.dot_general(a, b, (((1,), (1,)), ((), ())), preferred_element_type=F32)


def _dot_tn(a, b):
    return lax.dot_general(a, b, (((0,), (0,)), ((), ())), preferred_element_type=F32)


def _sigmoid(x):
    return 1.0 / (1.0 + jnp.exp(-x))


def _ada_kernel(c_ref, w_ref, b_ref, o_ref):
    c = c_ref[...]
    a = c * _sigmoid(c)
    o_ref[...] = jnp.dot(a, w_ref[...], preferred_element_type=F32, precision=HIGHEST) + b_ref[...]


def _ada(c, w_ada, b_ada3, layer, tn=1024):
    s, d = c.shape
    n = w_ada.shape[-1]
    return pl.pallas_call(
        _ada_kernel,
        grid=(n // tn,),
        in_specs=[
            pl.BlockSpec((s, d), lambda j: (0, 0)),
            pl.BlockSpec((None, d, tn), lambda j: (layer, 0, j)),
            pl.BlockSpec((None, 1, tn), lambda j: (layer, 0, j)),
        ],
        out_specs=pl.BlockSpec((s, tn), lambda j: (0, j)),
        out_shape=jax.ShapeDtypeStruct((s, n), F32),
        compiler_params=_params(("arbitrary",)),
        name="ada",
    )(c, w_ada, b_ada3)


def _rms(x):
    return x * lax.rsqrt(jnp.mean(x * x, axis=-1, keepdims=True) + NORM_EPS)


def _normmod_kernel(x_ref, g_ref, sh_ref, sc_ref, o_ref):
    y = _rms(x_ref[...]) * g_ref[...]
    o_ref[...] = (y * (1.0 + sc_ref[...]) + sh_ref[...]).astype(o_ref.dtype)


def _normmod(x, norm_g3, gi, mod3, seq0, rows_per_seq, j_shift, j_scale, tm=512):
    t, d = x.shape
    tm = min(tm, rows_per_seq)
    bps = rows_per_seq // tm
    return pl.pallas_call(
        _normmod_kernel,
        grid=(t // tm,),
        in_specs=[
            pl.BlockSpec((tm, d), lambda i: (i, 0)),
            pl.BlockSpec((None, 1, d), lambda i: (gi, 0, 0)),
            pl.BlockSpec((None, 1, d), lambda i: ((seq0 + i // bps) * N_MOD + j_shift, 0, 0)),
            pl.BlockSpec((None, 1, d), lambda i: ((seq0 + i // bps) * N_MOD + j_scale, 0, 0)),
        ],
        out_specs=pl.BlockSpec((tm, d), lambda i: (i, 0)),
        out_shape=jax.ShapeDtypeStruct((t, d), BF16),
        compiler_params=_params(("parallel",)),
        name="normmod",
    )(x, norm_g3, mod3, mod3)


def _norm_normmod_kernel(x_ref, g_ref, g2_ref, sh_ref, sc_ref, xn_ref, h_ref):
    xn = _rms(x_ref[...]) * g_ref[...]
    xn_ref[...] = xn
    y = _rms(xn) * g2_ref[...]
    h_ref[...] = (y * (1.0 + sc_ref[...]) + sh_ref[...]).astype(h_ref.dtype)


def _norm_normmod(x, norm_g3, gi, gi_next, mod3_next, seq0, rows_per_seq, tm=256):
    t, d = x.shape
    tm = min(tm, rows_per_seq)
    bps = rows_per_seq // tm
    row = pl.BlockSpec((tm, d), lambda i: (i, 0))
    return pl.pallas_call(
        _norm_normmod_kernel,
        grid=(t // tm,),
        in_specs=[
            row,
            pl.BlockSpec((None, 1, d), lambda i: (gi, 0, 0)),
            pl.BlockSpec((None, 1, d), lambda i: (gi_next, 0, 0)),
            pl.BlockSpec((None, 1, d), lambda i: ((seq0 + i // bps) * N_MOD + 0, 0, 0)),
            pl.BlockSpec((None, 1, d), lambda i: ((seq0 + i // bps) * N_MOD + 1, 0, 0)),
        ],
        out_specs=[row, row],
        out_shape=[jax.ShapeDtypeStruct((t, d), F32), jax.ShapeDtypeStruct((t, d), BF16)],
        compiler_params=_params(("parallel",)),
        name="norm_normmod",
    )(x, norm_g3, norm_g3, mod3_next, mod3_next)


def _norm_kernel(x_ref, g_ref, o_ref):
    o_ref[...] = _rms(x_ref[...]) * g_ref[...]


def _norm(x, norm_g3, gi, tm=512):
    t, d = x.shape
    tm = min(tm, t)
    return pl.pallas_call(
        _norm_kernel,
        grid=(t // tm,),
        in_specs=[
            pl.BlockSpec((tm, d), lambda i: (i, 0)),
            pl.BlockSpec((None, 1, d), lambda i: (gi, 0, 0)),
        ],
        out_specs=pl.BlockSpec((tm, d), lambda i: (i, 0)),
        out_shape=jax.ShapeDtypeStruct((t, d), F32),
        compiler_params=_params(("parallel",)),
        name="norm",
    )(x, norm_g3)


def _ffup_kernel(h_ref, wg_ref, wu_ref, o_ref):
    h = h_ref[...]
    g = _dot(h, wg_ref[...])
    u = _dot(h, wu_ref[...])
    o_ref[...] = (g * _sigmoid(g) * u).astype(o_ref.dtype)


def _ffup(h, wg, wu, li, tm=1024, tf=256):
    t, d = h.shape
    f = wg.shape[-1]
    tm, tf = min(tm, t), min(tf, f)
    return pl.pallas_call(
        _ffup_kernel,
        grid=(t // tm, f // tf),
        in_specs=[
            pl.BlockSpec((tm, d), lambda i, j: (i, 0)),
            pl.BlockSpec((None, d, tf), lambda i, j: (li, 0, j)),
            pl.BlockSpec((None, d, tf), lambda i, j: (li, 0, j)),
        ],
        out_specs=pl.BlockSpec((tm, tf), lambda i, j: (i, j)),
        out_shape=jax.ShapeDtypeStruct((t, f), BF16),
        compiler_params=_params(("parallel", "arbitrary")),
        name="ffup",
    )(h, wg, wu)


def _ffdown_kernel(a_ref, w_ref, x_ref, gate_ref, o_ref, *, scale):
    acc = _dot(a_ref[...], w_ref[...])
    o_ref[...] = x_ref[...] + (scale * gate_ref[...]) * acc


def _ffdown(act, wd, li, x, mod3, seq0, rows_per_seq, j_gate, scale, tm=512, tn=256):
    t, f = act.shape
    d = wd.shape[-1]
    tm, tn = min(tm, rows_per_seq), min(tn, d)
    bps = rows_per_seq // tm
    return pl.pallas_call(
        functools.partial(_ffdown_kernel, scale=scale),
        grid=(t // tm, d // tn),
        in_specs=[
            pl.BlockSpec((tm, f), lambda i, j: (i, 0)),
            pl.BlockSpec((None, f, tn), lambda i, j: (li, 0, j)),
            pl.BlockSpec((tm, tn), lambda i, j: (i, j)),
            pl.BlockSpec((None, 1, tn), lambda i, j: ((seq0 + i // bps) * N_MOD + j_gate, 0, j)),
        ],
        out_specs=pl.BlockSpec((tm, tn), lambda i, j: (i, j)),
        out_shape=jax.ShapeDtypeStruct((t, d), F32),
        compiler_params=_params(("parallel", "arbitrary")),
        name="ffdown",
    )(act, wd, x, mod3)


def _mm_kernel(a_ref, w_ref, o_ref):
    o_ref[...] = _dot(a_ref[...], w_ref[...]).astype(o_ref.dtype)


def _mm_in(h, w_in, layer, out_dtype, tm=1024, tn=512):
    t, d = h.shape
    n = w_in.shape[-1]
    tm, tn = min(tm, t), min(tn, n)
    return pl.pallas_call(
        _mm_kernel,
        grid=(t // tm, n // tn),
        in_specs=[
            pl.BlockSpec((tm, d), lambda i, j: (i, 0)),
            pl.BlockSpec((None, d, tn), lambda i, j: (layer, 0, j)),
        ],
        out_specs=pl.BlockSpec((tm, tn), lambda i, j: (i, j)),
        out_shape=jax.ShapeDtypeStruct((t, n), out_dtype),
        compiler_params=_params(("parallel", "arbitrary")),
        name="mm_in",
    )(h, w_in)


def _mixout_kernel(da_ref, hg_ref, s5_ref, w_ref, x_ref, gate_ref, o_ref, *, w_da, w_hg):
    acc = _dot(da_ref[...], w_ref[0:w_da, :])
    acc = acc + _dot(hg_ref[...], w_ref[w_da:w_da + w_hg, :])
    acc = acc + _dot(s5_ref[...], w_ref[w_da + w_hg:, :])
    o_ref[...] = x_ref[...] + gate_ref[...] * acc


def _mixout(o_da, o_hg, o_s5, w_out, layer, x, mod3, seq0, rows_per_seq, j_gate, tm=1024, tn=512):
    t, w_da = o_da.shape
    w_hg, w_s5 = o_hg.shape[1], o_s5.shape[1]
    d_in, d = w_out.shape[1], w_out.shape[2]
    tm, tn = min(tm, rows_per_seq), min(tn, d)
    bps = rows_per_seq // tm
    return pl.pallas_call(
        functools.partial(_mixout_kernel, w_da=w_da, w_hg=w_hg),
        grid=(t // tm, d // tn),
        in_specs=[
            pl.BlockSpec((tm, w_da), lambda i, j: (i, 0)),
            pl.BlockSpec((tm, w_hg), lambda i, j: (i, 0)),
            pl.BlockSpec((tm, w_s5), lambda i, j: (i, 0)),
            pl.BlockSpec((None, d_in, tn), lambda i, j: (layer, 0, j)),
            pl.BlockSpec((tm, tn), lambda i, j: (i, j)),
            pl.BlockSpec((None, 1, tn), lambda i, j: ((seq0 + i // bps) * N_MOD + j_gate, 0, j)),
        ],
        out_specs=pl.BlockSpec((tm, tn), lambda i, j: (i, j)),
        out_shape=jax.ShapeDtypeStruct((t, d), F32),
        compiler_params=_params(("parallel", "arbitrary")),
        name="mixout",
    )(o_da, o_hg, o_s5, w_out, x, mod3)


def _glu_kernel(y_ref, w_ref, b_ref, yt_ref, o_ref):
    acc = _dot(y_ref[...], w_ref[...]) + b_ref[...]
    o_ref[...] = (yt_ref[...].astype(F32) * _sigmoid(acc)).astype(o_ref.dtype)


def _glu(y, w_glu, b_glu3, layer, tm=1024, tn=512):
    t, d = y.shape
    tm, tn = min(tm, t), min(tn, d)
    return pl.pallas_call(
        _glu_kernel,
        grid=(t // tm, d // tn),
        in_specs=[
            pl.BlockSpec((tm, d), lambda i, j: (i, 0)),
            pl.BlockSpec((None, d, tn), lambda i, j: (layer, 0, j)),
            pl.BlockSpec((None, 1, tn), lambda i, j: (layer, 0, j)),
            pl.BlockSpec((tm, tn), lambda i, j: (i, j)),
        ],
        out_specs=pl.BlockSpec((tm, tn), lambda i, j: (i, j)),
        out_shape=jax.ShapeDtypeStruct((t, d), BF16),
        compiler_params=_params(("parallel", "arbitrary")),
        name="glu",
    )(y, w_glu, b_glu3, y)


def _qk_kernel(z_ref, g_ref, c_ref, s1_ref, s2_ref, o_ref):
    c, s1, s2, g = c_ref[...], s1_ref[...], s2_ref[...], g_ref[...]
    lane = lax.broadcasted_iota(jnp.int32, c.shape, 1)
    lo = lane < DA_SUB_DIM
    half = ROT_DIM // 2
    for h in range(DA_HEADS):
        cols = slice(h * V7X_LANES, (h + 1) * V7X_LANES)
        x = z_ref[:, cols].astype(F32)
        x2 = x * x
        s_lo = jnp.sum(jnp.where(lo, x2, 0.0), axis=-1, keepdims=True)
        s_hi = jnp.sum(jnp.where(lo, 0.0, x2), axis=-1, keepdims=True)
        ms = jnp.where(lo, s_lo, s_hi) * (1.0 / DA_SUB_DIM)
        y = x * lax.rsqrt(ms + NORM_EPS) * g
        y_next = pltpu.roll(y, V7X_LANES - half, axis=1)
        y_prev = pltpu.roll(y, half, axis=1)
        o_ref[:, cols] = (y * c + y_next * s1 + y_prev * s2).astype(o_ref.dtype)


def _qk_prep(za, gains, rope_c, rope_s1, rope_s2, seq_len, tm=512):
    t = za.shape[0]
    tm = min(tm, seq_len)
    assert seq_len % tm == 0 and t % seq_len == 0
    bps = seq_len // tm
    w = DA_HEADS * DA_HEAD_DIM
    tab = pl.BlockSpec((tm, V7X_LANES), lambda i, j: (i % bps, 0))
    return pl.pallas_call(
        _qk_kernel,
        grid=(t // tm, 2),
        in_specs=[
            pl.BlockSpec((tm, w), lambda i, j: (i, j)),
            pl.BlockSpec((None, 1, V7X_LANES), lambda i, j: (j, 0, 0)),
            tab, tab, tab,
        ],
        out_specs=pl.BlockSpec((tm, w), lambda i, j: (i, j)),
        out_shape=jax.ShapeDtypeStruct((t, 2 * w), BF16),
        compiler_params=_params(("parallel", "arbitrary")),
        name="qk_prep",
    )(za, gains, rope_c, rope_s1, rope_s2)


def _attn_kernel(q_ref, k_ref, v_ref, lam_ref, g_ref, o_ref, s_ref, m_ref, l_ref, acc_ref, *, tk, nk):
    q = q_ref[...]
    tq = q.shape[0]
    w = V7X_LANES
    nl = tk // w
    lane = lax.broadcasted_iota(jnp.int32, q.shape, 1)
    lo = lane < DA_SUB_DIM
    zero = jnp.zeros_like(q)
    q2 = jnp.concatenate([jnp.where(lo, q, zero), jnp.where(lo, zero, q)], axis=0)

    m_ref[...] = jnp.full(m_ref.shape, -jnp.inf, F32)

    def pass1(j, carry):
        rows = pl.ds(pl.multiple_of(j * tk, tk), tk)
        s = _dot_nt(q2, k_ref[rows, :])
        s_ref[j] = s
        mx = s[:, 0:w]
        for a in range(1, nl):
            mx = jnp.maximum(mx, s[:, a * w:(a + 1) * w])
        m_ref[...] = jnp.maximum(m_ref[...], mx)
        return carry

    unroll = 4 if nk % 8 == 0 else (2 if nk % 2 == 0 else 1)
    lax.fori_loop(0, nk, pass1, 0, unroll=unroll)
    m_ref[...] = jnp.broadcast_to(jnp.max(m_ref[...], axis=-1, keepdims=True), m_ref.shape)

    l_ref[...] = jnp.zeros(l_ref.shape, F32)
    acc_ref[...] = jnp.zeros(acc_ref.shape, F32)

    def pass2(j, carry):
        rows = pl.ds(pl.multiple_of(j * tk, tk), tk)
        s = s_ref[j]
        mb = m_ref[...]
        ps = [jnp.exp(s[:, a * w:(a + 1) * w] - mb) for a in range(nl)]
        lsum = ps[0]
        for a in range(1, nl):
            lsum = lsum + ps[a]
        l_ref[...] = l_ref[...] + lsum
        p = jnp.concatenate([x.astype(BF16) for x in ps], axis=1)
        acc_ref[...] = acc_ref[...] + _dot(p, v_ref[rows, :])
        return carry

    lax.fori_loop(0, nk, pass2, 0, unroll=unroll)
    acc = acc_ref[...]
    l = jnp.sum(l_ref[...], axis=-1, keepdims=True)
    o = acc[:tq] / l[:tq] - lam_ref[...] * (acc[tq:] / l[tq:])
    o_ref[...] = (_rms(o) * g_ref[...]).astype(o_ref.dtype)


def _attention(qk, za, lam_vec, subln_g, batch, seq_len, v_blk0, tq=256, tk=512):
    t = qk.shape[0]
    tq, tk = min(tq, seq_len), min(tk, seq_len)
    assert seq_len % tq == 0 and seq_len % tk == 0 and t == batch * seq_len
    nq, nk = seq_len // tq, seq_len // tk
    return pl.pallas_call(
        functools.partial(_attn_kernel, tk=tk, nk=nk),
        grid=(batch, DA_HEADS, nq),
        in_specs=[
            pl.BlockSpec((tq, V7X_LANES), lambda b, h, i: (b * nq + i, h)),
            pl.BlockSpec((seq_len, V7X_LANES), lambda b, h, i: (b, DA_HEADS + h)),
            pl.BlockSpec((seq_len, V7X_LANES), lambda b, h, i: (b, v_blk0 + h)),
            pl.BlockSpec((1, V7X_LANES), lambda b, h, i: (0, 0)),
            pl.BlockSpec((1, V7X_LANES), lambda b, h, i: (0, 0)),
        ],
        out_specs=pl.BlockSpec((tq, V7X_LANES), lambda b, h, i: (b * nq + i, h)),
        out_shape=jax.ShapeDtypeStruct((t, DA_HEADS * DA_HEAD_DIM), BF16),
        scratch_shapes=[
            pltpu.VMEM((nk, 2 * tq, tk), F32),
            pltpu.VMEM((2 * tq, V7X_LANES), F32),
            pltpu.VMEM((2 * tq, V7X_LANES), F32),
            pltpu.VMEM((2 * tq, V7X_LANES), F32),
        ],
        compiler_params=_params(("parallel", "parallel", "arbitrary")),
        name="diff_attn",
    )(qk, qk, za, lam_vec, subln_g)


def _bcast_rows(cum, idxs, rows_each):
    parts = [jnp.broadcast_to(cum[i:i + 1, :], (rows_each, cum.shape[1])) for i in idxs]
    return parts[0] if len(parts) == 1 else jnp.concatenate(parts, axis=0)


def _split3(x):
    hi = x.astype(BF16)
    r1 = x - hi.astype(F32)
    mid = r1.astype(BF16)
    lo = (r1 - mid.astype(F32)).astype(BF16)
    return hi, mid, lo


def _gla_masks(c, rev):
    t = np.arange(c)[:, None]
    s = np.arange(c)[None, :]
    vis = (s >= t) if rev else (s <= t)
    level = np.full((c, c), -1, np.int32)
    blk, k = c, int(math.log2(c // GLA_DIAG))
    while blk >= GLA_DIAG:
        level = np.where(vis & (t // blk == s // blk), k, level)
        blk //= 2
        k -= 1
    return jnp.asarray(vis, BF16), jnp.asarray(level, jnp.int32)


def _gla_chunk(q, k, v, lf, st, rev, trib, level):
    c = q.shape[0]
    rowf = lax.broadcasted_iota(jnp.int32, q.shape, 0)
    hi, mid, lo = _split3(lf)
    cum = (_dot(trib, hi) + _dot(trib, mid)) + _dot(trib, lo)
    edge = cum[0:1, :] if rev else cum[c - 1:c, :]

    q_in = (q * jnp.exp(cum)).astype(BF16)
    k_st = (k * jnp.exp(edge - cum)).astype(BF16)
    vb = v.astype(BF16)
    o = _dot_nt(q_in, st.astype(BF16))
    st_new = st * jnp.exp(edge) + _dot_tn(vb, k_st)

    d = GLA_DIAG
    idxs = [j * d + (d // 2 if rev else d // 2 - 1) for j in range(c // d)]
    ref = _bcast_rows(cum, idxs, d)
    a = _dot_nt((q * jnp.exp(cum - ref)).astype(BF16), (k * jnp.exp(ref - cum)).astype(BF16))
    att = jnp.where(level == 0, a, 0.0)
    m, lev = d, 1
    while m < c:
        blk = 2 * m
        idxs = [j * blk + (m if rev else m - 1) for j in range(c // blk)]
        bnd = _bcast_rows(cum, idxs, blk)
        late = (rowf & (blk - 1)) >= m
        q_side = jnp.logical_not(late) if rev else late
        e = jnp.exp(jnp.where(q_side, cum - bnd, bnd - cum))
        a = _dot_nt((q * e).astype(BF16), (k * e).astype(BF16))
        att = jnp.where(level == lev, a, att)
        m, lev = blk, lev + 1
    o = o + _dot(att.astype(BF16), vb)
    return o, st_new


def _hgrn_kernel(qh_ref, ih_ref, gh_ref, zf_ref, zb_ref, lbf_ref, lbb_ref, gn_ref,
                 trif_ref, trib_ref, lvf_ref, lvb_ref, o_ref, of_ref, ob_ref, st_ref, *, chunk, nc, heads, ft):
    w = V7X_LANES

    def gates(z, lb):
        lbc = jnp.clip(lb, 0.0, 1.0 - LB_FLOOR)
        e = jnp.exp(-jnp.abs(z))
        r = 1.0 / (1.0 + e)
        pos = z >= 0.0
        sig = jnp.where(pos, r, e * r)
        nsig = jnp.where(pos, e * r, r)
        logf = jnp.minimum(jnp.log(jnp.maximum(lbc, LB_FLOOR) + (1.0 - lbc) * sig), 0.0)
        return logf, (1.0 - lbc) * nsig

    def load(rows, cols, z_ref, lb_ref):
        x = qh_ref[rows, cols].astype(F32)
        q = x * _sigmoid(x) * (HG_DIM ** -0.5)
        lf, k = gates(z_ref[rows, cols], lb_ref[:, cols])
        return q, k, ih_ref[rows, cols].astype(F32), lf

    st_ref[...] = jnp.zeros(st_ref.shape, F32)

    def step(i, carry):
        rf = pl.ds(pl.multiple_of(i * chunk, chunk), chunk)
        rb = pl.ds(pl.multiple_of((nc - 1 - i) * chunk, chunk), chunk)
        for hh in range(heads):
            cols = slice(hh * w, (hh + 1) * w)
            q, k, v, lf = load(rf, cols, zf_ref, lbf_ref)
            o, st = _gla_chunk(q, k, v, lf, st_ref[2 * hh], False, trif_ref[...], lvf_ref[...])
            st_ref[2 * hh] = st
            of_ref[rf, cols] = o
            q, k, v, lf = load(rb, cols, zb_ref, lbb_ref)
            o, st = _gla_chunk(q, k, v, lf, st_ref[2 * hh + 1], True, trib_ref[...], lvb_ref[...])
            st_ref[2 * hh + 1] = st
            ob_ref[rb, cols] = o
        return carry

    lax.fori_loop(0, nc, step, 0, unroll=2 if nc % 2 == 0 else 1)

    def finish(i, carry):
        rows = pl.ds(pl.multiple_of(i * ft, ft), ft)
        for hh in range(heads):
            cols = slice(hh * w, (hh + 1) * w)
            o = of_ref[rows, cols] + ob_ref[rows, cols]
            g = gh_ref[rows, cols].astype(F32)
            o_ref[rows, cols] = (_rms(o) * gn_ref[...] * (g * _sigmoid(g))).astype(o_ref.dtype)
        return carry

    lax.fori_loop(0, (nc * chunk) // ft, finish, 0)


def _hgrn(za, zg, lb_f, lb_b, gn_g, batch, seq_len, q_blk0, i_blk0, g_blk0, chunk=GLA_CHUNK, heads=2):
    t = za.shape[0]
    chunk = min(chunk, seq_len)
    ft = min(512, seq_len)
    hw = heads * V7X_LANES
    npair = HG_HEADS // heads

    def spec(blk0):
        return pl.BlockSpec((seq_len, hw), lambda b, p: (b, blk0 // heads + p))

    lbspec = pl.BlockSpec((None, 1, hw), lambda b, p: (p, 0, 0))
    cc = pl.BlockSpec((chunk, chunk), lambda b, p: (0, 0))
    tri_f, lv_f = _gla_masks(chunk, False)
    tri_b, lv_b = _gla_masks(chunk, True)
    return pl.pallas_call(
        functools.partial(_hgrn_kernel, chunk=chunk, nc=seq_len // chunk, heads=heads, ft=ft),
        grid=(batch, npair),
        in_specs=[spec(q_blk0), spec(i_blk0), spec(g_blk0), spec(0), spec(HG_HEADS), lbspec, lbspec,
                  pl.BlockSpec((1, V7X_LANES), lambda b, p: (0, 0)), cc, cc, cc, cc],
        out_specs=pl.BlockSpec((seq_len, hw), lambda b, p: (b, p)),
        out_shape=jax.ShapeDtypeStruct((t, HG_HEADS * HG_DIM), BF16),
        scratch_shapes=[pltpu.VMEM((seq_len, hw), F32), pltpu.VMEM((seq_len, hw), F32),
                        pltpu.VMEM((2 * heads, HG_DIM, HG_DIM), F32)],
        compiler_params=_params(("parallel", "parallel")),
        name="hgrn2",
    )(za, za, za, zg, zg, lb_f, lb_b, gn_g, tri_f, tri_b, lv_f, lv_b)


def _gelu_tanh(x):
    return 0.5 * x * (1.0 + jnp.tanh(0.7978845608028654 * (x + 0.044715 * (x * x * x))))


def _s5_kernel(u_ref, t_ref, f_ref, e_ref, a_ref, y_ref, fs_ref, xs_ref, *, nc, nseq, tr):
    m = nc * nseq
    for r0 in range(0, m, tr):
        fs_ref[r0:r0 + tr, :] = _dot(u_ref[r0:r0 + tr, :], f_ref[...])

    w = V7X_LANES
    a = a_ref[...]
    afr, afi, abr, abi = (jnp.broadcast_to(a[i:i + 1, :], (nseq, w)) for i in range(4))

    def body(i, carry):
        xfr, xfi, xbr, xbi = carry
        rf = pl.ds(pl.multiple_of(i * nseq, nseq), nseq)
        rb = pl.ds(pl.multiple_of((nc - 1 - i) * nseq, nseq), nseq)
        xs_ref[rf, 0:w] = xfr
        xs_ref[rf, w:2 * w] = xfi
        xs_ref[rb, 2 * w:3 * w] = xbr
        xs_ref[rb, 3 * w:4 * w] = xbi
        nfr = afr * xfr - afi * xfi + fs_ref[rf, 0:w]
        nfi = afr * xfi + afi * xfr + fs_ref[rf, w:2 * w]
        nbr = abr * xbr - abi * xbi + fs_ref[rb, 2 * w:3 * w]
        nbi = abr * xbi + abi * xbr + fs_ref[rb, 3 * w:4 * w]
        return nfr, nfi, nbr, nbi

    z = jnp.zeros((nseq, w), F32)
    lax.fori_loop(0, nc, body, (z, z, z, z))

    for r0 in range(0, m, tr):
        y = _dot(u_ref[r0:r0 + tr, :], t_ref[...]) + _dot(xs_ref[r0:r0 + tr, :].astype(BF16), e_ref[...])
        y_ref[r0:r0 + tr, :] = _gelu_tanh(y).astype(y_ref.dtype)


def _s5(u, t_tab, f_tab, e_tab, a_tab, nc, nseq, tr=512):
    noct, m, ow = u.shape
    npair = t_tab.shape[0]
    pw = t_tab.shape[1]
    ppo = ow // pw
    sw = f_tab.shape[2]
    tr = min(tr, m)
    uspec = pl.BlockSpec((None, m, pw), lambda i: (i // ppo, 0, i % ppo))
    return pl.pallas_call(
        functools.partial(_s5_kernel, nc=nc, nseq=nseq, tr=tr),
        grid=(npair,),
        in_specs=[
            uspec,
            pl.BlockSpec((None, pw, pw), lambda i: (i, 0, 0)),
            pl.BlockSpec((None, pw, sw), lambda i: (i, 0, 0)),
            pl.BlockSpec((None, sw, pw), lambda i: (i, 0, 0)),
            pl.BlockSpec((None, 4, V7X_LANES), lambda i: (i, 0, 0)),
        ],
        out_specs=uspec,
        out_shape=jax.ShapeDtypeStruct(u.shape, BF16),
        scratch_shapes=[pltpu.VMEM((m, sw), F32), pltpu.VMEM((m, sw), F32)],
        compiler_params=_params(("parallel",)),
        name="s5",
    )(u, t_tab, f_tab, e_tab, a_tab)


def _s5_in_kernel(x_ref, perm_ref, o_ref, *, cs, noct, b0, nt0, nt1):
    b, j = pl.program_id(0), pl.program_id(1)
    valid = j < jnp.where(b < b0, nt0, nt1)
    w = V7X_LANES
    full = noct * w

    @pl.when(valid)
    def _():
        for o in range(noct):
            xc = jnp.concatenate([x_ref[:, s * full + o * w:s * full + (o + 1) * w] for s in range(cs)], axis=1)
            o_ref[o] = _dot(xc, perm_ref[...]).astype(o_ref.dtype)

    @pl.when(jnp.logical_not(valid))
    def _():
        o_ref[...] = jnp.zeros(o_ref.shape, o_ref.dtype)


def _s5_out_kernel(y_ref, perm_ref, o_ref, *, cs, noct):
    w = V7X_LANES
    full = noct * w
    for o in range(noct):
        yc = _dot_nt(y_ref[o], perm_ref[...]).astype(o_ref.dtype)
        for s in range(cs):
            o_ref[:, s * full + o * w:s * full + (o + 1) * w] = yc[:, s * w:(s + 1) * w]


def _s5_block_maps(seqs, tr):
    (b0, n0), (b1, n1) = seqs
    nt0, nt1 = n0 // tr, n1 // tr

    def tiles(b):
        return jnp.where(b < b0, nt0, nt1)

    def row_block(b, j):
        return jnp.where(b < b0, b * nt0 + j, b0 * nt0 + (b - b0) * nt1 + j)

    return b0, nt0, nt1, tiles, row_block


def _s5_in(zs, perm, seqs, cs, tr):
    b0, nt0, nt1, tiles, row_block = _s5_block_maps(seqs, tr)
    nseq, ntm = seqs[0][0] + seqs[1][0], max(nt0, nt1)
    noct = zs.shape[1] // (cs * V7X_LANES)
    ow = cs * V7X_LANES
    return pl.pallas_call(
        functools.partial(_s5_in_kernel, cs=cs, noct=noct, b0=b0, nt0=nt0, nt1=nt1),
        grid=(nseq, ntm),
        in_specs=[
            pl.BlockSpec((tr, zs.shape[1]), lambda b, j: (row_block(b, jnp.minimum(j, tiles(b) - 1)), 0)),
            pl.BlockSpec((ow, ow), lambda b, j: (0, 0)),
        ],
        out_specs=pl.BlockSpec((noct, tr, ow), lambda b, j: (0, j, b)),
        out_shape=jax.ShapeDtypeStruct((noct, ntm * tr, nseq * ow), BF16),
        compiler_params=_params(("parallel", "arbitrary")),
        name="s5_in",
    )(zs, perm)


def _s5_out(y, perm, seqs, cs, tr):
    _, nt0, nt1, _, _ = _s5_block_maps(seqs, tr)
    nseq, ntm = seqs[0][0] + seqs[1][0], max(nt0, nt1)
    noct = y.shape[0]
    ow = cs * V7X_LANES
    return pl.pallas_call(
        functools.partial(_s5_out_kernel, cs=cs, noct=noct),
        grid=(nseq, ntm),
        in_specs=[
            pl.BlockSpec((noct, tr, ow), lambda b, j: (0, j, b)),
            pl.BlockSpec((ow, ow), lambda b, j: (0, 0)),
        ],
        out_specs=pl.BlockSpec((tr, noct * ow), lambda b, j: (b * ntm + j, 0)),
        out_shape=jax.ShapeDtypeStruct((nseq * ntm * tr, noct * ow), BF16),
        compiler_params=_params(("parallel", "arbitrary")),
        name="s5_out",
    )(y, perm)


def _s5_perm(cs):
    src = np.arange(cs * V7X_LANES)
    s, g, n = src // V7X_LANES, (src % V7X_LANES) // S5_GROUP, src % S5_GROUP
    dst = g * (cs * S5_GROUP) + s * S5_GROUP + n
    p = np.zeros((cs * V7X_LANES, cs * V7X_LANES), np.float32)
    p[src, dst] = 1.0
    return jnp.asarray(p, BF16)


def _s5_tables(lam_re, lam_im, log_dt, b_re, b_im, c_re, c_im, d_skip):
    cs, n, p, g = S5_CHUNK, S5_GROUP, S5_STATE, S5_GROUPS
    lr = jnp.minimum(lam_re.astype(F32), -1e-4)
    li = lam_im.astype(F32)
    dt = jnp.exp(log_dt.astype(F32))[..., None]
    mag = jnp.exp(lr * dt)
    ar, ai = mag * jnp.cos(li * dt), mag * jnp.sin(li * dt)
    den = lr * lr + li * li
    er, ei = ar - 1.0, ai
    zr, zi = (er * lr + ei * li) / den, (ei * lr - er * li) / den
    br, bi = b_re.astype(F32), b_im.astype(F32)
    bbr = zr[..., None] * br - zi[..., None] * bi
    bbi = zr[..., None] * bi + zi[..., None] * br
    cr, ci = c_re.astype(F32), c_im.astype(F32)
    tau = jnp.arange(cs + 1, dtype=F32)[:, None, None, None]
    pmag = jnp.exp(lr[None] * dt[None] * tau)
    pr = pmag * jnp.cos(li[None] * dt[None] * tau)
    pi = pmag * jnp.sin(li[None] * dt[None] * tau)
    car = cr[None] * pr[:, :, :, None, :] - ci[None] * pi[:, :, :, None, :]
    cai = cr[None] * pi[:, :, :, None, :] + ci[None] * pr[:, :, :, None, :]
    kern = (jnp.einsum('tdgmp,dgpn->tdgnm', car, bbr, precision=HIGHEST)
            - jnp.einsum('tdgmp,dgpn->tdgnm', cai, bbi, precision=HIGHEST))
    s_in = jnp.arange(cs)[:, None]
    s_out = jnp.arange(cs)[None, :]
    lag = s_out - s_in
    kf = kern[jnp.clip(lag, 0, cs), 0]
    kb = kern[jnp.clip(-lag, 0, cs), 1]
    tt = (jnp.where((lag >= 0)[:, :, None, None, None], kf, 0.0)
          + jnp.where((lag <= 0)[:, :, None, None, None], kb, 0.0))
    eye_s = jnp.eye(cs, dtype=F32)[:, :, None, None, None]
    eye_n = jnp.eye(n, dtype=F32)[None, None, None]
    tt = tt + eye_s * eye_n * d_skip.astype(F32).reshape(g, n)[None, None, :, :, None]
    t_g = jnp.transpose(tt, (2, 0, 3, 1, 4)).reshape(g, cs * n, cs * n)
    pf_r, pf_i = pr[cs - 1 - jnp.arange(cs), 0], pi[cs - 1 - jnp.arange(cs), 0]
    pb_r, pb_i = pr[jnp.arange(cs), 1], pi[jnp.arange(cs), 1]

    def cmul_b(qr, qi, d):
        re = qr[:, :, :, None] * bbr[d][None] - qi[:, :, :, None] * bbi[d][None]
        im = qr[:, :, :, None] * bbi[d][None] + qi[:, :, :, None] * bbr[d][None]
        return jnp.transpose(re, (1, 0, 3, 2)), jnp.transpose(im, (1, 0, 3, 2))

    ffr, ffi = cmul_b(pf_r, pf_i, 0)
    fbr, fbi = cmul_b(pb_r, pb_i, 1)
    f_g = jnp.stack([ffr, ffi, fbr, fbi], axis=3).reshape(g, cs * n, 4, p)
    ef_r, ef_i = car[1 + jnp.arange(cs), 0], cai[1 + jnp.arange(cs), 0]
    eb_r, eb_i = car[cs - jnp.arange(cs), 1], cai[cs - jnp.arange(cs), 1]
    e_g = jnp.stack([ef_r, -ef_i, eb_r, -eb_i], axis=0)
    e_g = jnp.transpose(e_g, (2, 0, 4, 1, 3)).reshape(g, 4, p, cs * n)
    eye2 = jnp.eye(2, dtype=F32)
    w = cs * n
    t_p = (t_g.reshape(g // 2, 2, w, 1, w) * eye2[None, :, None, :, None]).reshape(g // 2, 2 * w, 2 * w)
    f_p = (f_g.reshape(g // 2, 2, w, 4, 1, p) * eye2[None, :, None, None, :, None]).reshape(g // 2, 2 * w, 8 * p)
    e_p = (jnp.transpose(e_g.reshape(g // 2, 2, 4, p, w), (0, 2, 1, 3, 4))[:, :, :, :, None, :]
           * eye2[None, None, :, None, :, None]).reshape(g // 2, 8 * p, 2 * w)
    a_p = jnp.stack([pr[cs, 0], pi[cs, 0], pr[cs, 1], pi[cs, 1]], axis=1).reshape(g // 2, 2, 4, p)
    a_p = jnp.transpose(a_p, (0, 2, 1, 3)).reshape(g // 2, 4, 2 * p)
    return t_p.astype(BF16), f_p.astype(BF16), e_p.astype(BF16), a_p


def _rope_tables(seq_len):
    half = ROT_DIM // 2
    inv = ROPE_THETA ** (-jnp.arange(0, ROT_DIM, 2, dtype=F32) / ROT_DIM)
    ang = jnp.arange(seq_len, dtype=F32)[:, None] * inv[None, :]
    cos, sin = jnp.cos(ang), jnp.sin(ang)
    ones = jnp.ones((seq_len, DA_SUB_DIM - ROT_DIM), F32)
    zeros = jnp.zeros((seq_len, DA_SUB_DIM - ROT_DIM), F32)
    zh = jnp.zeros((seq_len, half), F32)
    c = jnp.concatenate([cos, cos, ones], axis=1)
    s1 = jnp.concatenate([-sin, zh, zeros], axis=1)
    s2 = jnp.concatenate([zh, sin, zeros], axis=1)
    rep = DA_HEAD_DIM // DA_SUB_DIM
    return jnp.tile(c, (1, rep)), jnp.tile(s1, (1, rep)), jnp.tile(s2, (1, rep))


def kernel(x_prompt, x_sample, c_prompt, c_sample, w_ada, b_ada, norm_g, w_ff_gate, w_ff_up, w_ff_down, w_in, w_out, qk_norm_g, diff_lambda, diff_subln_g, hg_lb_logits, hg_gnorm_g, s5_lam_re, s5_lam_im, s5_log_dt, s5_b_re, s5_b_im, s5_c_re, s5_c_im, s5_d, w_glu, b_glu):
    depth, d = norm_g.shape[0], norm_g.shape[2]
    d_ff = w_ff_gate.shape[-1]
    trunks = []
    seq0 = 0
    for xin in (x_prompt, x_sample):
        b, l, _ = xin.shape
        trunks.append(dict(x=xin.reshape(b * l, d), b=b, l=l, seq0=seq0, rope=_rope_tables(l)))
        seq0 += b
    nseq = seq0
    c_all = jnp.concatenate([c_prompt, c_sample], axis=0)

    wg = w_ff_gate.astype(BF16).reshape(depth * 2, d, d_ff)
    wu = w_ff_up.astype(BF16).reshape(depth * 2, d, d_ff)
    wd = w_ff_down.astype(BF16).reshape(depth * 2, d_ff, d)
    n_da, n_hg = 3 * DA_HEADS * DA_HEAD_DIM, HG_HEADS * HG_DIM
    g0, g1 = n_da + n_hg, n_da + 3 * n_hg
    w_in_a = jnp.concatenate([w_in[..., :g0], w_in[..., g1:]], axis=-1).astype(BF16)
    w_in_g = w_in[..., g0:g1].astype(BF16)
    w_out_b = w_out.astype(BF16)
    w_glu_b = w_glu.astype(BF16)
    b_ada3 = b_ada.reshape(depth, 1, -1)
    b_glu3 = b_glu.reshape(depth, 1, -1)
    norm_g3 = norm_g.reshape(depth * 4, 1, d)

    lbs = jax.nn.softmax(hg_lb_logits.astype(F32), axis=1)
    lbs = jnp.cumsum(lbs, axis=1) - lbs[:, :1]
    s5_seqs = tuple((tr["b"], tr["l"] // S5_CHUNK) for tr in trunks)
    nc_s5 = max(n for _, n in s5_seqs)
    s5_tr = min(S5_ROW_TILE, min(n for _, n in s5_seqs))
    s5_perm = _s5_perm(S5_CHUNK)
    v_blk0 = 2 * DA_HEADS
    hq_blk0, hi_blk0, hgt_blk0 = 3 * DA_HEADS, 3 * DA_HEADS + HG_HEADS, 3 * DA_HEADS + 2 * HG_HEADS
    s5_col0 = (3 * DA_HEADS + 3 * HG_HEADS) * V7X_LANES
    hg_pair = 2

    mods = [_ada(c_all, w_ada, b_ada3, layer).reshape(nseq * N_MOD, 1, d) for layer in range(depth)]

    for layer in range(depth):
        mod3 = mods[layer]
        lam_init = 0.8 - 0.6 * math.exp(-0.3 * layer)
        lv = diff_lambda[layer].astype(F32)
        lam = jnp.exp(jnp.sum(lv[0] * lv[1])) - jnp.exp(jnp.sum(lv[2] * lv[3])) + lam_init
        lam_vec = jnp.full((1, V7X_LANES), lam, F32)
        subln = (diff_subln_g[layer].astype(F32) * (1.0 - lam_init)).reshape(1, DA_HEAD_DIM)
        rep = DA_HEAD_DIM // DA_SUB_DIM
        gq = jnp.tile(qk_norm_g[layer, 0].astype(F32), rep) * (DA_SUB_DIM ** -0.5)
        gk = jnp.tile(qk_norm_g[layer, 1].astype(F32), rep)
        gains = jnp.stack([gq, gk], axis=0)[:, None, :]
        lb_f = lbs[0, layer].reshape(HG_HEADS // hg_pair, 1, hg_pair * HG_DIM)
        lb_b = lbs[1, layer].reshape(HG_HEADS // hg_pair, 1, hg_pair * HG_DIM)
        gn_g = hg_gnorm_g[layer].astype(F32).reshape(1, HG_DIM)
        s5_tabs = _s5_tables(s5_lam_re[layer], s5_lam_im[layer], s5_log_dt[layer], s5_b_re[layer],
                             s5_b_im[layer], s5_c_re[layer], s5_c_im[layer], s5_d[layer])

        zs_list = []
        for tr in trunks:
            b, l, s0 = tr["b"], tr["l"], tr["seq0"]
            x = tr["x"]
            h = tr.pop("h", None)
            if h is None:
                h = _normmod(x, norm_g3, layer * 4 + 0, mod3, s0, l, 0, 1)
            act = _ffup(h, wg, wu, layer * 2 + 0)
            x = _ffdown(act, wd, layer * 2 + 0, x, mod3, s0, l, 2, 0.5)
            h = _normmod(x, norm_g3, layer * 4 + 1, mod3, s0, l, 3, 4)
            za = _mm_in(h, w_in_a, layer, BF16)
            zg = _mm_in(h, w_in_g, layer, F32)
            qk = _qk_prep(za, gains, *tr["rope"], l)
            tr["o_da"] = _attention(qk, za, lam_vec, subln, b, l, v_blk0)
            tr["o_hg"] = _hgrn(za, zg, lb_f, lb_b, gn_g, b, l, hq_blk0, hi_blk0, hgt_blk0, heads=hg_pair)
            zs_list.append(za[:, s5_col0:])
            tr["x"] = x

        zs = jnp.concatenate(zs_list, axis=0)
        s5_w = zs.shape[1]
        u = _s5_in(zs.reshape(-1, S5_CHUNK * s5_w), s5_perm, s5_seqs, S5_CHUNK, s5_tr)
        noct = u.shape[0]
        y_all = _s5(u.reshape(noct, nc_s5 * nseq, -1), *s5_tabs, nc_s5, nseq)
        y_all = _s5_out(y_all.reshape(u.shape), s5_perm, s5_seqs, S5_CHUNK, s5_tr)
        y_all = y_all.reshape(nseq, nc_s5 * S5_CHUNK, s5_w)

        for tr in trunks:
            b, l, s0 = tr["b"], tr["l"], tr["seq0"]
            x = tr["x"]
            y = y_all[s0:s0 + b, :l].reshape(b * l, s5_w)
            o_s5 = _glu(y, w_glu_b, b_glu3, layer)
            x = _mixout(tr["o_da"], tr["o_hg"], o_s5, w_out_b, layer, x, mod3, s0, l, 5)
            h = _normmod(x, norm_g3, layer * 4 + 2, mod3, s0, l, 6, 7)
            act = _ffup(h, wg, wu, layer * 2 + 1)
            x = _ffdown(act, wd, layer * 2 + 1, x, mod3, s0, l, 8, 0.5)
            if layer + 1 < depth:
                tr["x"], tr["h"] = _norm_normmod(x, norm_g3, layer * 4 + 3, layer * 4 + 4, mods[layer + 1], s0, l)
            else:
                tr["x"] = _norm(x, norm_g3, layer * 4 + 3)

    return tuple(tr["x"].reshape(tr["b"], tr["l"], d) for tr in trunks)
```
